```python
import functools
import jax, jax.numpy as jnp
from jax import lax
import numpy as np

D_MODEL = 1024
BATCH = 8
SEQ = 2048
DEPTH = 1
DEC_BATCH = 128
DEC_SEQ = 4
PAST_LEN = 16384
PAGE_SIZE = 128

N_HEADS = 16
N_KV_HEADS = 2
HEAD_DIM = 64
GROUP = N_HEADS // N_KV_HEADS
WINDOW = 128
ROPE_THETA = 10000.0
CHUNK = 128
A_GROUPS = 4
A_GROUP_DIM = 128
A_WIDTH = A_GROUPS * A_GROUP_DIM
D_FF = 2816
LN_EPS = 1e-5
NEG_INF = -1e30
DN_ALPHA = (2.0 * DEPTH) ** 0.25
DN_BETA = (8.0 * DEPTH) ** -0.25
Q_WIDTH = N_HEADS * HEAD_DIM
KV_WIDTH = N_KV_HEADS * HEAD_DIM
SPLIT_POINTS = [A_WIDTH, 2 * A_WIDTH, 2 * A_WIDTH + Q_WIDTH, 2 * A_WIDTH + Q_WIDTH + KV_WIDTH,
                2 * A_WIDTH + Q_WIDTH + 2 * KV_WIDTH, 2 * A_WIDTH + Q_WIDTH + 2 * KV_WIDTH + D_MODEL]
IN_WIDTH = 2 * A_WIDTH + Q_WIDTH + 2 * KV_WIDTH + 2 * D_MODEL

kernel_name = 'gated_gmlp_swa_sink_macaron_deepnorm_step'


def _layernorm(x, g, b):
    xf = x.astype(jnp.float32)
    mu = xf.mean(-1, keepdims=True)
    var = jnp.square(xf - mu).mean(-1, keepdims=True)
    y = (xf - mu) * lax.rsqrt(var + LN_EPS) * g.astype(jnp.float32) + b.astype(jnp.float32)
    return y.astype(x.dtype)


def _swiglu(x, w_up, w_down):
    g, u = jnp.split(x @ w_up, 2, axis=-1)
    return (jax.nn.silu(g) * u) @ w_down


def _rope(x, pos):
    half = HEAD_DIM // 2
    inv = ROPE_THETA ** (-jnp.arange(half, dtype=jnp.float32) / half)
    ang = pos.astype(jnp.float32)[:, None] * inv[None, :]
    cos = jnp.cos(ang)[:, None, :]
    sin = jnp.sin(ang)[:, None, :]
    xf = x.astype(jnp.float32)
    x1, x2 = xf[..., :half], xf[..., half:]
    return jnp.concatenate([x1 * cos - x2 * sin, x2 * cos + x1 * sin], axis=-1).astype(x.dtype)


def _sink_attend(q, k, v, mask, sinks):
    s = jnp.einsum('...qkgd,...skd->...kgqs', q, k).astype(jnp.float32) * (HEAD_DIM ** -0.5)
    s = jnp.where(mask, s, NEG_INF)
    sk = sinks.astype(jnp.float32).reshape(N_KV_HEADS, GROUP, 1, 1)
    m = jnp.maximum(s.max(-1, keepdims=True), sk)
    p = jnp.exp(s - m)
    w = p / (p.sum(-1, keepdims=True) + jnp.exp(sk - m))
    return jnp.einsum('...kgqs,...skd->...qkgd', w.astype(v.dtype), v)


def _attend_prompt(q, k, v, sinks, buf):
    B, L = q.shape[0], q.shape[1]
    nb = L // WINDOW
    qb = q.reshape(B, nb, WINDOW, N_KV_HEADS, GROUP, HEAD_DIM)

    def two_blocks(t):
        tb = t.reshape(B, nb, WINDOW, N_KV_HEADS, HEAD_DIM)
        prev = jnp.pad(tb, ((0, 0), (1, 0), (0, 0), (0, 0), (0, 0)))[:, :nb]
        return jnp.concatenate([prev, tb], axis=2)

    i = jnp.arange(WINDOW)[:, None]
    j = jnp.arange(2 * WINDOW)[None, :]
    d = i + WINDOW - j
    band = (d >= 0) & (d < WINDOW)
    valid = (jnp.arange(nb) > 0)[:, None, None] | (j >= WINDOW)[None]
    mask = (band[None] & valid)[None, :, None, None]
    out = _sink_attend(qb, two_blocks(k), two_blocks(v), mask, sinks)
    return out.reshape(B, L, Q_WIDTH), k[:, L - buf:], v[:, L - buf:]


def _attend_sample(q, k, v, sinks, k_buf, v_buf):
    B, T = q.shape[0], q.shape[1]
    buf = k_buf.shape[1]
    kk = jnp.concatenate([k_buf.astype(k.dtype), k], axis=1)
    vv = jnp.concatenate([v_buf.astype(v.dtype), v], axis=1)
    qpos = PAST_LEN + jnp.arange(T)
    kpos = PAST_LEN - buf + jnp.arange(buf + T)
    d = qpos[:, None] - kpos[None, :]
    mask = (d >= 0) & (d < WINDOW)
    out = _sink_attend(q, kk, vv, mask, sinks)
    return out.reshape(B, T, Q_WIDTH), kk[:, T:], vv[:, T:]


def _chunk_gmlp(u, v, ln_g, ln_b, ws, bs):
    B, L = u.shape[0], u.shape[1]
    vn = _layernorm(v, ln_g, ln_b)
    nc = -(-L // CHUNK)
    lp = nc * CHUNK
    vc = jnp.pad(vn, ((0, 0), (0, lp - L), (0, 0))).reshape(B, nc, CHUNK, A_GROUPS, A_GROUP_DIM)
    causal = jnp.tril(jnp.ones((CHUNK, CHUNK), dtype=bool))
    wm = jnp.where(causal[None], ws, jnp.zeros_like(ws)).astype(vc.dtype)
    mixed = jnp.einsum('gts,bnsgc->bntgc', wm, vc) + bs.T[:, :, None].astype(vc.dtype)
    mixed = mixed.reshape(B, lp, A_WIDTH)[:, :L]
    start = ((L - 1) // CHUNK) * CHUNK
    return u * mixed, vn[:, start:]


def _layer(x, pos, attend, p):
    (ffn1_up, ffn1_down, ln1_g, ln1_b, w_in, a_ln_g, a_ln_b, a_ws, a_bs, attn_sinks,
     w_pa, w_pb, w_o, ln2_g, ln2_b, ffn2_up, ffn2_down, ln3_g, ln3_b) = p
    B, L = x.shape[0], x.shape[1]
    h = _layernorm(DN_ALPHA * x + 0.5 * _swiglu(x, ffn1_up, ffn1_down), ln1_g, ln1_b)
    a_u, a_v, q, k, v, g_a, g_b = jnp.split(h @ w_in, SPLIT_POINTS, axis=-1)
    a_out, a_state = _chunk_gmlp(jax.nn.gelu(a_u), jax.nn.gelu(a_v), a_ln_g, a_ln_b, a_ws, a_bs)
    q = _rope(q.reshape(B, L, N_HEADS, HEAD_DIM), pos).reshape(B, L, N_KV_HEADS, GROUP, HEAD_DIM)
    k = _rope(k.reshape(B, L, N_KV_HEADS, HEAD_DIM), pos)
    v = v.reshape(B, L, N_KV_HEADS, HEAD_DIM)
    b_out, k_state, v_state = attend(q, k, v, attn_sinks)
    merged = jax.nn.sigmoid(g_a) * (a_out @ w_pa) + jax.nn.sigmoid(g_b) * (b_out @ w_pb)
    h = _layernorm(DN_ALPHA * h + merged @ w_o, ln2_g, ln2_b)
    h = _layernorm(DN_ALPHA * h + 0.5 * _swiglu(h, ffn2_up, ffn2_down), ln3_g, ln3_b)
    return h, k_state, v_state, a_state


def setup_inputs(seed: int = 0) -> dict:
    key = jax.random.key(seed)
    ks = jax.random.split(key, 32)
    f32 = jnp.float32
    buf = min(WINDOW, PAST_LEN)

    def nrm(k, shape, scale):
        return jax.random.normal(k, shape, f32) * scale

    def gain(k, n):
        return 1.0 + nrm(k, (DEPTH, n), 0.02)

    return {
        'x_prompt': nrm(ks[0], (BATCH, SEQ, D_MODEL), 1.0),
        'x_sample': nrm(ks[1], (DEC_BATCH, DEC_SEQ, D_MODEL), 1.0),
        'cache_win_k': nrm(ks[2], (DEPTH, DEC_BATCH, buf, N_KV_HEADS, HEAD_DIM), 1.0),
        'cache_win_v': nrm(ks[3], (DEPTH, DEC_BATCH, buf, N_KV_HEADS, HEAD_DIM), 1.0),
        'ffn1_up': nrm(ks[4], (DEPTH, D_MODEL, 2 * D_FF), D_MODEL ** -0.5),
        'ffn1_down': nrm(ks[5], (DEPTH, D_FF, D_MODEL), DN_BETA * D_FF ** -0.5),
        'ln1_g': gain(ks[6], D_MODEL),
        'ln1_b': nrm(ks[7], (DEPTH, D_MODEL), 0.02),
        'w_in': nrm(ks[8], (DEPTH, D_MODEL, IN_WIDTH), D_MODEL ** -0.5),
        'a_ln_g': gain(ks[9], A_WIDTH),
        'a_ln_b': nrm(ks[10], (DEPTH, A_WIDTH), 0.02),
        'a_ws': nrm(ks[11], (DEPTH, A_GROUPS, CHUNK, CHUNK), CHUNK ** -0.5),
        'a_bs': 1.0 + nrm(ks[12], (DEPTH, A_GROUPS, CHUNK), 0.02),
        'attn_sinks': nrm(ks[13], (DEPTH, N_HEADS), 0.5),
        'w_pa': nrm(ks[14], (DEPTH, A_WIDTH, D_MODEL), A_WIDTH ** -0.5),
        'w_pb': nrm(ks[15], (DEPTH, Q_WIDTH, D_MODEL), Q_WIDTH ** -0.5),
        'w_o': nrm(ks[16], (DEPTH, D_MODEL, D_MODEL), DN_BETA * D_MODEL ** -0.5),
        'ln2_g': gain(ks[17], D_MODEL),
        'ln2_b': nrm(ks[18], (DEPTH, D_MODEL), 0.02),
        'ffn2_up': nrm(ks[19], (DEPTH, D_MODEL, 2 * D_FF), D_MODEL ** -0.5),
        'ffn2_down': nrm(ks[20], (DEPTH, D_FF, D_MODEL), DN_BETA * D_FF ** -0.5),
        'ln3_g': gain(ks[21], D_MODEL),
        'ln3_b': nrm(ks[22], (DEPTH, D_MODEL), 0.02),
    }


def reference(x_prompt, x_sample, cache_win_k, cache_win_v, ffn1_up, ffn1_down, ln1_g, ln1_b,
              w_in, a_ln_g, a_ln_b, a_ws, a_bs, attn_sinks, w_pa, w_pb, w_o, ln2_g, ln2_b,
              ffn2_up, ffn2_down, ln3_g, ln3_b):
    buf = cache_win_k.shape[2]
    weights = (ffn1_up, ffn1_down, ln1_g, ln1_b, w_in, a_ln_g, a_ln_b, a_ws, a_bs, attn_sinks,
               w_pa, w_pb, w_o, ln2_g, ln2_b, ffn2_up, ffn2_down, ln3_g, ln3_b)
    pos_p = jnp.arange(x_prompt.shape[1])
    pos_s = PAST_LEN + jnp.arange(x_sample.shape[1])
    hp, hs = x_prompt, x_sample
    kp_l, vp_l, ks_l, vs_l, ap_l, as_l = [], [], [], [], [], []
    for l in range(DEPTH):
        p = tuple(w[l] for w in weights)
        hp, kp, vp, ap = _layer(hp, pos_p, functools.partial(_attend_prompt, buf=buf), p)
        hs, ks_, vs_, as_ = _layer(hs, pos_s, functools.partial(_attend_sample, k_buf=cache_win_k[l], v_buf=cache_win_v[l]), p)
        kp_l.append(kp); vp_l.append(vp); ap_l.append(ap)
        ks_l.append(ks_); vs_l.append(vs_); as_l.append(as_)
    win_k_prompt = jnp.stack(kp_l)
    win_v_prompt = jnp.stack(vp_l)
    win_k_sample = jnp.stack(ks_l)
    win_v_sample = jnp.stack(vs_l)
    chunk_v_prompt = jnp.stack(ap_l)
    chunk_v_sample = jnp.stack(as_l)
    return (hp, hs, win_k_prompt, win_v_prompt, win_k_sample, win_v_sample, chunk_v_prompt, chunk_v_sample)
```

```python
import functools

import numpy as np
import jax
import jax.numpy as jnp
from jax import lax
from jax.experimental import pallas as pl
from jax.experimental.pallas import tpu as pltpu

F32 = jnp.float32
BF16 = jnp.bfloat16

D_MODEL = 1024
DEPTH = 1
PAST_LEN = 16384
N_HEADS = 16
N_KV_HEADS = 2
HEAD_DIM = 64
GROUP = N_HEADS // N_KV_HEADS
WINDOW = 128
ROPE_THETA = 10000.0
CHUNK = 128
A_GROUPS = 4
A_GROUP_DIM = 128
A_WIDTH = A_GROUPS * A_GROUP_DIM
D_FF = 2816
LN_EPS = 1e-5
NEG_INF = -1e30
DN_ALPHA = (2.0 * DEPTH) ** 0.25
Q_WIDTH = N_HEADS * HEAD_DIM
KV_WIDTH = N_KV_HEADS * HEAD_DIM
MIX_WIDTH = 2 * A_WIDTH + Q_WIDTH + 2 * KV_WIDTH
IN_WIDTH = MIX_WIDTH + 2 * D_MODEL

LANES = 128
VMEM_LIMIT_BYTES = 56 * 1024 * 1024

ROW_TILE = 512
FF_CHUNK = 256
SAMPLE_KEY_PAD = 16
SAMPLE_BATCH_TILE = 8


def _resident(shape):
    nd = len(shape)
    return pl.BlockSpec(shape, lambda *_: (0,) * nd, pipeline_mode=pl.Buffered(1))


def _layernorm(y, g, b):
    mu = jnp.mean(y, axis=-1, keepdims=True)
    d = y - mu
    var = jnp.mean(d * d, axis=-1, keepdims=True)
    return d * lax.rsqrt(var + LN_EPS) * g + b


def _ffn_ln_kernel(x_ref, wup_ref, wdn_ref, g_ref, b_ref, o_ref):
    x = x_ref[...].astype(F32)
    xb = x.astype(BF16)
    y = None
    for c in range(D_FF // FF_CHUNK):
        lo = c * FF_CHUNK
        gate = jnp.dot(xb, wup_ref[:, lo:lo + FF_CHUNK], preferred_element_type=F32)
        up = jnp.dot(xb, wup_ref[:, D_FF + lo:D_FF + lo + FF_CHUNK], preferred_element_type=F32)
        act = (gate * jax.nn.sigmoid(gate) * up).astype(BF16)
        part = jnp.dot(act, wdn_ref[lo:lo + FF_CHUNK, :], preferred_element_type=F32)
        y = part if y is None else y + part
    r = DN_ALPHA * x + 0.5 * y
    o_ref[...] = _layernorm(r, g_ref[...], b_ref[...]).astype(o_ref.dtype)


def _ffn_ln(x, w_up, w_down, g, b, *, tm):
    t = x.shape[0]
    row = pl.BlockSpec((tm, D_MODEL), lambda i: (i, 0))
    return pl.pallas_call(
        _ffn_ln_kernel,
        grid=(t // tm,),
        in_specs=[row, _resident(w_up.shape), _resident(w_down.shape),
                  _resident(g.shape), _resident(b.shape)],
        out_specs=row,
        out_shape=jax.ShapeDtypeStruct((t, D_MODEL), F32),
        compiler_params=pltpu.CompilerParams(
            dimension_semantics=("arbitrary",), vmem_limit_bytes=VMEM_LIMIT_BYTES),
        name="ffn_ln",
    )(x, w_up, w_down, g, b)


def _rope(x, cos, sin_signed):
    lane = lax.broadcasted_iota(jnp.int32, x.shape, 1)
    first_half = (lane & (HEAD_DIM - 1)) < (HEAD_DIM // 2)
    partner = jnp.where(first_half, pltpu.roll(x, LANES - HEAD_DIM // 2, axis=1),
                        pltpu.roll(x, HEAD_DIM // 2, axis=1))
    return x * cos + partner * sin_signed


def _in_proj_kernel(h_ref, w_ref, lng_ref, lnb_ref, mix_ref, bias_ref, cos_ref, sin_ref,
                    a_ref, vn_ref, q_ref, k_ref, v_ref):
    tm = h_ref.shape[0]
    hb = h_ref[...].astype(BF16)
    cos = cos_ref[...]
    sin = sin_ref[...]

    gu = jax.nn.gelu(jnp.dot(hb, w_ref[:, 0:A_WIDTH], preferred_element_type=F32))
    gv = jax.nn.gelu(jnp.dot(hb, w_ref[:, A_WIDTH:2 * A_WIDTH], preferred_element_type=F32))
    vn = _layernorm(gv, lng_ref[...], lnb_ref[...])
    vn_ref[...] = vn
    vnb = vn.astype(BF16)
    r = lax.broadcasted_iota(jnp.int32, (CHUNK, CHUNK), 0)
    c = lax.broadcasted_iota(jnp.int32, (CHUNK, CHUNK), 1)
    causal = c <= r
    for g in range(A_GROUPS):
        cols = slice(g * A_GROUP_DIM, (g + 1) * A_GROUP_DIM)
        wm = jnp.where(causal, mix_ref[g], 0.0).astype(BF16)
        for ch in range(tm // CHUNK):
            rows = slice(ch * CHUNK, (ch + 1) * CHUNK)
            mixed = jnp.dot(wm, vnb[rows, cols], preferred_element_type=F32) + bias_ref[:, cols]
            a_ref[rows, cols] = (gu[rows, cols] * mixed).astype(a_ref.dtype)

    q = jnp.dot(hb, w_ref[:, 2 * A_WIDTH:2 * A_WIDTH + Q_WIDTH], preferred_element_type=F32)
    for j in range(Q_WIDTH // LANES):
        cols = slice(j * LANES, (j + 1) * LANES)
        q_ref[:, cols] = (_rope(q[:, cols], cos, sin) * (HEAD_DIM ** -0.5)).astype(q_ref.dtype)
    kv = jnp.dot(hb, w_ref[:, 2 * A_WIDTH + Q_WIDTH:MIX_WIDTH], preferred_element_type=F32)
    k_ref[...] = _rope(kv[:, :KV_WIDTH], cos, sin)
    v_ref[...] = kv[:, KV_WIDTH:]


def _in_proj(h, w_mix, a_ln_g, a_ln_b, mix, bias_full, cos_t, sin_t, *, tm, table_blocks):
    t = h.shape[0]
    row = lambda width: pl.BlockSpec((tm, width), lambda i: (i, 0))
    table = pl.BlockSpec((tm, LANES), lambda i: (i % table_blocks, 0))
    return pl.pallas_call(
        _in_proj_kernel,
        grid=(t // tm,),
        in_specs=[row(D_MODEL), _resident(w_mix.shape), _resident(a_ln_g.shape), _resident(a_ln_b.shape),
                  _resident(mix.shape), _resident(bias_full.shape), table, table],
        out_specs=[row(A_WIDTH), row(A_WIDTH), row(Q_WIDTH), row(KV_WIDTH), row(KV_WIDTH)],
        out_shape=[jax.ShapeDtypeStruct((t, A_WIDTH), BF16), jax.ShapeDtypeStruct((t, A_WIDTH), F32),
                   jax.ShapeDtypeStruct((t, Q_WIDTH), BF16), jax.ShapeDtypeStruct((t, KV_WIDTH), F32),
                   jax.ShapeDtypeStruct((t, KV_WIDTH), F32)],
        compiler_params=pltpu.CompilerParams(
            dimension_semantics=("arbitrary",), vmem_limit_bytes=VMEM_LIMIT_BYTES),
        name="in_proj",
    )(h, w_mix, a_ln_g, a_ln_b, mix, bias_full, cos_t, sin_t)


def _split_heads(x2, x2_rolled, kv_head):
    lane = lax.broadcasted_iota(jnp.int32, x2.shape, 1)
    low = lane < HEAD_DIM
    zero = jnp.zeros_like(x2)
    if kv_head == 0:
        return jnp.where(low, x2, zero), jnp.where(low, zero, x2_rolled)
    return jnp.where(low, x2_rolled, zero), jnp.where(low, zero, x2)


def _attend(q_slab, keys, vals, mask, sink):
    s = lax.dot_general(q_slab, keys, (((1,), (1,)), ((), ())), preferred_element_type=F32)
    s = jnp.where(mask, s, NEG_INF)
    m = jnp.maximum(jnp.max(s, axis=-1, keepdims=True), sink)
    p = jnp.exp(s - m)
    denom = jnp.sum(p, axis=-1, keepdims=True) + jnp.exp(sink - m)
    o = jnp.dot(p.astype(BF16), vals, preferred_element_type=F32)
    return o / denom


def _attn_prompt_kernel(sink_ref, q_ref, kp_ref, ko_ref, vp_ref, vo_ref, o_ref):
    has_prev = pl.program_id(1) > 0
    k2 = jnp.concatenate([kp_ref[...], ko_ref[...]], axis=0)
    v2 = jnp.concatenate([vp_ref[...], vo_ref[...]], axis=0)
    k2r = pltpu.roll(k2, HEAD_DIM, axis=1)
    v2r = pltpu.roll(v2, HEAD_DIM, axis=1)
    row = lax.broadcasted_iota(jnp.int32, (WINDOW, 2 * WINDOW), 0)
    col = lax.broadcasted_iota(jnp.int32, (WINDOW, 2 * WINDOW), 1)
    first_row = row + jnp.where(has_prev, 0, 2 * WINDOW)
    mask = ((col < WINDOW) & (col > first_row)) | ((col >= WINDOW) & ((col - WINDOW) <= row))
    for kh in range(N_KV_HEADS):
        keys = [x.astype(BF16) for x in _split_heads(k2, k2r, kh)]
        vals = [x.astype(BF16) for x in _split_heads(v2, v2r, kh)]
        for slab in range(GROUP // 2):
            cols = slice((kh * GROUP // 2 + slab) * LANES, (kh * GROUP // 2 + slab + 1) * LANES)
            q_slab = q_ref[:, cols]
            out = None
            for parity in range(2):
                sink = sink_ref[kh * GROUP + 2 * slab + parity]
                o = _attend(q_slab, keys[parity], vals[parity], mask, sink)
                out = o if out is None else out + o
            o_ref[:, cols] = out.astype(o_ref.dtype)


def _attn_prompt(sinks, q, k, v, *, batch, blocks):
    t = q.shape[0]
    own = lambda b, i: (b * blocks + i, 0)
    prev = lambda b, i: (b * blocks + jnp.maximum(i - 1, 0), 0)
    kv_spec = lambda imap: pl.BlockSpec((WINDOW, KV_WIDTH), imap)
    return pl.pallas_call(
        _attn_prompt_kernel,
        grid=(batch, blocks),
        in_specs=[pl.BlockSpec(memory_space=pltpu.SMEM),
                  pl.BlockSpec((WINDOW, Q_WIDTH), own),
                  kv_spec(prev), kv_spec(own), kv_spec(prev), kv_spec(own)],
        out_specs=pl.BlockSpec((WINDOW, Q_WIDTH), own),
        out_shape=jax.ShapeDtypeStruct((t, Q_WIDTH), BF16),
        compiler_params=pltpu.CompilerParams(
            dimension_semantics=("arbitrary", "arbitrary"), vmem_limit_bytes=VMEM_LIMIT_BYTES),
        name="attn_prompt",
    )(sinks, q, k, k, v, v)


def _attn_sample_kernel(sink_ref, q_ref, kn_ref, vn_ref, kc_ref, vc_ref, o_ref, wk_ref, wv_ref, *, dec_seq):
    rows = (GROUP // 2) * dec_seq
    n_keys = 2 * WINDOW
    key_pad = jnp.zeros((WINDOW - SAMPLE_KEY_PAD, KV_WIDTH), F32)
    token = lax.broadcasted_iota(jnp.int32, (rows, n_keys), 0) % dec_seq
    col = lax.broadcasted_iota(jnp.int32, (rows, n_keys), 1)
    mask = ((col < WINDOW) & (col > token)) | ((col >= WINDOW) & ((col - WINDOW) <= token))
    for b in range(q_ref.shape[0]):
        kc = kc_ref[b]
        vc = vc_ref[b]
        kn = kn_ref[b]
        vn = vn_ref[b]
        wk_ref[b, 0:WINDOW - dec_seq, :] = kc[dec_seq:, :]
        wk_ref[b, WINDOW - dec_seq:WINDOW, :] = kn[0:dec_seq, :]
        wv_ref[b, 0:WINDOW - dec_seq, :] = vc[dec_seq:, :]
        wv_ref[b, WINDOW - dec_seq:WINDOW, :] = vn[0:dec_seq, :]
        k2 = jnp.concatenate([kc, kn, key_pad], axis=0)
        v2 = jnp.concatenate([vc, vn, key_pad], axis=0)
        k2r = pltpu.roll(k2, HEAD_DIM, axis=1)
        v2r = pltpu.roll(v2, HEAD_DIM, axis=1)
        for kh in range(N_KV_HEADS):
            keys = [x.astype(BF16) for x in _split_heads(k2, k2r, kh)]
            vals = [x.astype(BF16) for x in _split_heads(v2, v2r, kh)]
            q_rows = q_ref[b, kh * rows:(kh + 1) * rows, :]
            slab = lax.broadcasted_iota(jnp.int32, (rows, 1), 0) // dec_seq
            out = None
            for parity in range(2):
                sink = jnp.zeros((rows, 1), F32)
                for s in range(GROUP // 2):
                    sink = jnp.where(slab == s, sink_ref[kh * GROUP + 2 * s + parity], sink)
                o = _attend(q_rows, keys[parity], vals[parity], mask, sink)
                out = o if out is None else out + o
            o_ref[b, kh * rows:(kh + 1) * rows, :] = out.astype(o_ref.dtype)


def _attn_sample(sinks, q, k_new, v_new, k_cache, v_cache, *, dec_seq):
    nb = q.shape[0]
    bt = SAMPLE_BATCH_TILE
    blk = lambda rows: pl.BlockSpec((bt, rows, LANES), lambda i: (i, 0, 0))
    q_rows = N_KV_HEADS * (GROUP // 2) * dec_seq
    return pl.pallas_call(
        functools.partial(_attn_sample_kernel, dec_seq=dec_seq),
        grid=(nb // bt,),
        in_specs=[pl.BlockSpec(memory_space=pltpu.SMEM), blk(q_rows), blk(SAMPLE_KEY_PAD), blk(SAMPLE_KEY_PAD),
                  blk(WINDOW), blk(WINDOW)],
        out_specs=[blk(q_rows), blk(WINDOW), blk(WINDOW)],
        out_shape=[jax.ShapeDtypeStruct((nb, q_rows, LANES), BF16),
                   jax.ShapeDtypeStruct((nb, WINDOW, KV_WIDTH), F32),
                   jax.ShapeDtypeStruct((nb, WINDOW, KV_WIDTH), F32)],
        compiler_params=pltpu.CompilerParams(
            dimension_semantics=("arbitrary",), vmem_limit_bytes=VMEM_LIMIT_BYTES),
        name="attn_sample",
    )(sinks, q, k_new, v_new, k_cache, v_cache)


def _merge_kernel(h_ref, a_ref, b_ref, wg_ref, wpa_ref, wpb_ref, wo_ref, g_ref, bb_ref, o_ref):
    h = h_ref[...]
    hb = h.astype(BF16)
    gate_a = jax.nn.sigmoid(jnp.dot(hb, wg_ref[:, :D_MODEL], preferred_element_type=F32))
    pa = jnp.dot(a_ref[...], wpa_ref[...], preferred_element_type=F32)
    merged = gate_a * pa
    gate_b = jax.nn.sigmoid(jnp.dot(hb, wg_ref[:, D_MODEL:], preferred_element_type=F32))
    pb = jnp.dot(b_ref[...], wpb_ref[...], preferred_element_type=F32)
    merged = (merged + gate_b * pb).astype(BF16)
    y = jnp.dot(merged, wo_ref[...], preferred_element_type=F32)
    o_ref[...] = _layernorm(DN_ALPHA * h + y, g_ref[...], bb_ref[...])


def _merge(h, a_out, b_out, w_gate, w_pa, w_pb, w_o, g, b, *, tm):
    t = h.shape[0]
    row = lambda width: pl.BlockSpec((tm, width), lambda i: (i, 0))
    return pl.pallas_call(
        _merge_kernel,
        grid=(t // tm,),
        in_specs=[row(D_MODEL), row(A_WIDTH), row(Q_WIDTH), _resident(w_gate.shape), _resident(w_pa.shape),
                  _resident(w_pb.shape), _resident(w_o.shape), _resident(g.shape), _resident(b.shape)],
        out_specs=row(D_MODEL),
        out_shape=jax.ShapeDtypeStruct((t, D_MODEL), F32),
        compiler_params=pltpu.CompilerParams(
            dimension_semantics=("arbitrary",), vmem_limit_bytes=VMEM_LIMIT_BYTES),
        name="merge",
    )(h, a_out, b_out, w_gate, w_pa, w_pb, w_o, g, b)


def _rope_tables(positions):
    half = HEAD_DIM // 2
    inv = ROPE_THETA ** (-np.arange(half, dtype=np.float64) / half)
    ang = np.asarray(positions, np.float64)[:, None] * inv[None, :]
    cos = np.tile(np.cos(ang), (1, 2 * LANES // HEAD_DIM))
    sin = np.tile(np.concatenate([-np.sin(ang), np.sin(ang)], axis=1), (1, LANES // HEAD_DIM))
    return jnp.asarray(cos, F32), jnp.asarray(sin, F32)


def kernel(x_prompt, x_sample, cache_win_k, cache_win_v, ffn1_up, ffn1_down, ln1_g, ln1_b, w_in, a_ln_g, a_ln_b,
           a_ws, a_bs, attn_sinks, w_pa, w_pb, w_o, ln2_g, ln2_b, ffn2_up, ffn2_down, ln3_g, ln3_b):
    batch, seq, _ = x_prompt.shape
    dec_batch, dec_seq, _ = x_sample.shape
    buf = cache_win_k.shape[2]
    assert DEPTH == 1 and buf == WINDOW and seq % ROW_TILE == 0 and ROW_TILE % CHUNK == 0
    assert CHUNK % dec_seq == 0 and (dec_batch * dec_seq) % CHUNK == 0 and dec_batch % SAMPLE_BATCH_TILE == 0
    tp, ts = batch * seq, dec_batch * dec_seq
    tm_s = min(ROW_TILE, ts)

    up1, dn1 = ffn1_up[0].astype(BF16), ffn1_down[0].astype(BF16)
    up2, dn2 = ffn2_up[0].astype(BF16), ffn2_down[0].astype(BF16)
    w_mix, w_gate = w_in[0, :, :MIX_WIDTH].astype(BF16), w_in[0, :, MIX_WIDTH:].astype(BF16)
    wpa, wpb, wo = w_pa[0].astype(BF16), w_pb[0].astype(BF16), w_o[0].astype(BF16)
    sinks = attn_sinks[0]

    mix_p = a_ws[0]
    bias_p = jnp.repeat(a_bs[0].T, A_GROUP_DIM, axis=1)
    reps = CHUNK // dec_seq
    eye = jnp.eye(reps, dtype=F32)
    mix_s = jnp.einsum("ab,gts->gatbs", eye, a_ws[0, :, :dec_seq, :dec_seq]).reshape(A_GROUPS, CHUNK, CHUNK)
    bias_s = jnp.tile(bias_p[:dec_seq], (reps, 1))

    cos_p, sin_p = _rope_tables(np.arange(seq))
    cos_s, sin_s = _rope_tables(PAST_LEN + (np.arange(tm_s) % dec_seq))

    hp = _ffn_ln(x_prompt.reshape(tp, D_MODEL), up1, dn1, ln1_g, ln1_b, tm=ROW_TILE)
    ap, vnp, qp, kp, vp = _in_proj(hp, w_mix, a_ln_g, a_ln_b, mix_p, bias_p, cos_p, sin_p,
                                   tm=ROW_TILE, table_blocks=seq // ROW_TILE)
    bp = _attn_prompt(sinks, qp, kp, vp, batch=batch, blocks=seq // WINDOW)
    hp2 = _merge(hp, ap, bp, w_gate, wpa, wpb, wo, ln2_g, ln2_b, tm=ROW_TILE)
    yp = _ffn_ln(hp2, up2, dn2, ln3_g, ln3_b, tm=ROW_TILE)

    hs = _ffn_ln(x_sample.reshape(ts, D_MODEL), up1, dn1, ln1_g, ln1_b, tm=tm_s)
    as_, vns, qs, ks, vs = _in_proj(hs, w_mix, a_ln_g, a_ln_b, mix_s, bias_s, cos_s, sin_s,
                                    tm=tm_s, table_blocks=1)
    slabs = Q_WIDTH // LANES
    qs_t = qs.reshape(dec_batch, dec_seq, slabs, LANES).transpose(0, 2, 1, 3).reshape(dec_batch, slabs * dec_seq, LANES)
    pad = ((0, 0), (0, SAMPLE_KEY_PAD - dec_seq), (0, 0))
    ks_pad = jnp.pad(ks.reshape(dec_batch, dec_seq, KV_WIDTH), pad)
    vs_pad = jnp.pad(vs.reshape(dec_batch, dec_seq, KV_WIDTH), pad)
    bs_t, win_k_s, win_v_s = _attn_sample(
        sinks, qs_t, ks_pad, vs_pad, cache_win_k[0].reshape(dec_batch, buf, KV_WIDTH),
        cache_win_v[0].reshape(dec_batch, buf, KV_WIDTH), dec_seq=dec_seq)
    bs_ = bs_t.reshape(dec_batch, slabs, dec_seq, LANES).transpose(0, 2, 1, 3).reshape(ts, Q_WIDTH)
    hs2 = _merge(hs, as_, bs_, w_gate, wpa, wpb, wo, ln2_g, ln2_b, tm=tm_s)
    ys = _ffn_ln(hs2, up2, dn2, ln3_g, ln3_b, tm=tm_s)

    kv_shape = (DEPTH, -1, buf, N_KV_HEADS, HEAD_DIM)
    start = ((seq - 1) // CHUNK) * CHUNK
    return (yp.reshape(batch, seq, D_MODEL), ys.reshape(dec_batch, dec_seq, D_MODEL),
            kp.reshape(batch, seq, KV_WIDTH)[:, seq - buf:].reshape(kv_shape),
            vp.reshape(batch, seq, KV_WIDTH)[:, seq - buf:].reshape(kv_shape),
            win_k_s.reshape(kv_shape), win_v_s.reshape(kv_shape),
            vnp.reshape(batch, seq, A_WIDTH)[None, :, start:],
            vns.reshape(DEPTH, dec_batch, dec_seq, A_WIDTH))
```

```python
import functools

import numpy as np
import jax
import jax.numpy as jnp
from jax import lax
from jax.experimental import pallas as pl
from jax.experimental.pallas import tpu as pltpu

F32 = jnp.float32
BF16 = jnp.bfloat16

D_MODEL = 1024
DEPTH = 1
PAST_LEN = 16384
N_HEADS = 16
N_KV_HEADS = 2
HEAD_DIM = 64
GROUP = N_HEADS // N_KV_HEADS
WINDOW = 128
ROPE_THETA = 10000.0
CHUNK = 128
A_GROUPS = 4
A_GROUP_DIM = 128
A_WIDTH = A_GROUPS * A_GROUP_DIM
D_FF = 2816
LN_EPS = 1e-5
NEG_INF = -1e30
DN_ALPHA = (2.0 * DEPTH) ** 0.25
LOG2E = 1.4426950408889634
Q_WIDTH = N_HEADS * HEAD_DIM
KV_WIDTH = N_KV_HEADS * HEAD_DIM
MIX_WIDTH = 2 * A_WIDTH + Q_WIDTH + 2 * KV_WIDTH
IN_WIDTH = MIX_WIDTH + 2 * D_MODEL

LANES = 128
VMEM_LIMIT_BYTES = 56 * 1024 * 1024

ROW_TILE = 512
FF_CHUNK = 256
SAMPLE_KEY_PAD = 16
SAMPLE_BATCH_TILE = 16
ATTN_Q_TILE = 512


def _resident(shape):
    nd = len(shape)
    return pl.BlockSpec(shape, lambda *_: (0,) * nd, pipeline_mode=pl.Buffered(1))


def _layernorm(y, g, b):
    mu = jnp.mean(y, axis=-1, keepdims=True)
    d = y - mu
    var = jnp.mean(d * d, axis=-1, keepdims=True)
    return d * lax.rsqrt(var + LN_EPS) * g + b


def _ffn_ln_kernel(x_ref, wup_ref, wdn_ref, g_ref, b_ref, o_ref):
    x = x_ref[...].astype(F32)
    xb = x.astype(BF16)
    y = None
    for c in range(D_FF // FF_CHUNK):
        lo = c * FF_CHUNK
        gate = jnp.dot(xb, wup_ref[:, lo:lo + FF_CHUNK], preferred_element_type=F32)
        up = jnp.dot(xb, wup_ref[:, D_FF + lo:D_FF + lo + FF_CHUNK], preferred_element_type=F32)
        act = (gate * jax.nn.sigmoid(gate) * up).astype(BF16)
        part = jnp.dot(act, wdn_ref[lo:lo + FF_CHUNK, :], preferred_element_type=F32)
        y = part if y is None else y + part
    r = DN_ALPHA * x + 0.5 * y
    o_ref[...] = _layernorm(r, g_ref[...], b_ref[...]).astype(o_ref.dtype)


def _ffn_ln(x, w_up, w_down, g, b, *, tm):
    t = x.shape[0]
    row = pl.BlockSpec((tm, D_MODEL), lambda i: (i, 0))
    return pl.pallas_call(
        _ffn_ln_kernel,
        grid=(t // tm,),
        in_specs=[row, _resident(w_up.shape), _resident(w_down.shape),
                  _resident(g.shape), _resident(b.shape)],
        out_specs=row,
        out_shape=jax.ShapeDtypeStruct((t, D_MODEL), F32),
        compiler_params=pltpu.CompilerParams(
            dimension_semantics=("arbitrary",), vmem_limit_bytes=VMEM_LIMIT_BYTES),
        name="ffn_ln",
    )(x, w_up, w_down, g, b)


def _rope(x, cos, sin_signed):
    lane = lax.broadcasted_iota(jnp.int32, x.shape, 1)
    first_half = (lane & (HEAD_DIM - 1)) < (HEAD_DIM // 2)
    partner = jnp.where(first_half, pltpu.roll(x, LANES - HEAD_DIM // 2, axis=1),
                        pltpu.roll(x, HEAD_DIM // 2, axis=1))
    return x * cos + partner * sin_signed


def _in_proj_kernel(h_ref, w_ref, lng_ref, lnb_ref, mix_ref, bias_ref, cos_ref, sin_ref,
                    a_ref, vn_ref, q_ref, k_ref, v_ref):
    tm = h_ref.shape[0]
    hb = h_ref[...].astype(BF16)
    cos = cos_ref[...]
    sin = sin_ref[...]

    gu = jax.nn.gelu(jnp.dot(hb, w_ref[:, 0:A_WIDTH], preferred_element_type=F32))
    gv = jax.nn.gelu(jnp.dot(hb, w_ref[:, A_WIDTH:2 * A_WIDTH], preferred_element_type=F32))
    vn = _layernorm(gv, lng_ref[...], lnb_ref[...])
    vn_ref[...] = vn
    vnb = vn.astype(BF16)
    r = lax.broadcasted_iota(jnp.int32, (CHUNK, CHUNK), 0)
    c = lax.broadcasted_iota(jnp.int32, (CHUNK, CHUNK), 1)
    causal = c <= r
    for g in range(A_GROUPS):
        cols = slice(g * A_GROUP_DIM, (g + 1) * A_GROUP_DIM)
        wm = jnp.where(causal, mix_ref[g], 0.0).astype(BF16)
        for ch in range(tm // CHUNK):
            rows = slice(ch * CHUNK, (ch + 1) * CHUNK)
            mixed = jnp.dot(wm, vnb[rows, cols], preferred_element_type=F32) + bias_ref[:, cols]
            a_ref[rows, cols] = (gu[rows, cols] * mixed).astype(a_ref.dtype)

    q = jnp.dot(hb, w_ref[:, 2 * A_WIDTH:2 * A_WIDTH + Q_WIDTH], preferred_element_type=F32)
    for j in range(Q_WIDTH // LANES):
        cols = slice(j * LANES, (j + 1) * LANES)
        q_ref[:, cols] = (_rope(q[:, cols], cos, sin) * (LOG2E * HEAD_DIM ** -0.5)).astype(q_ref.dtype)
    kv = jnp.dot(hb, w_ref[:, 2 * A_WIDTH + Q_WIDTH:MIX_WIDTH], preferred_element_type=F32)
    k_ref[...] = _rope(kv[:, :KV_WIDTH], cos, sin)
    v_ref[...] = kv[:, KV_WIDTH:]


def _in_proj(h, w_mix, a_ln_g, a_ln_b, mix, bias_full, cos_t, sin_t, *, tm, table_blocks):
    t = h.shape[0]
    row = lambda width: pl.BlockSpec((tm, width), lambda i: (i, 0))
    table = pl.BlockSpec((tm, LANES), lambda i: (i % table_blocks, 0))
    return pl.pallas_call(
        _in_proj_kernel,
        grid=(t // tm,),
        in_specs=[row(D_MODEL), _resident(w_mix.shape), _resident(a_ln_g.shape), _resident(a_ln_b.shape),
                  _resident(mix.shape), _resident(bias_full.shape), table, table],
        out_specs=[row(A_WIDTH), row(A_WIDTH), row(Q_WIDTH), row(KV_WIDTH), row(KV_WIDTH)],
        out_shape=[jax.ShapeDtypeStruct((t, A_WIDTH), BF16), jax.ShapeDtypeStruct((t, A_WIDTH), F32),
                   jax.ShapeDtypeStruct((t, Q_WIDTH), BF16), jax.ShapeDtypeStruct((t, KV_WIDTH), F32),
                   jax.ShapeDtypeStruct((t, KV_WIDTH), F32)],
        compiler_params=pltpu.CompilerParams(
            dimension_semantics=("arbitrary",), vmem_limit_bytes=VMEM_LIMIT_BYTES),
        name="in_proj",
    )(h, w_mix, a_ln_g, a_ln_b, mix, bias_full, cos_t, sin_t)


def _kv_tiles(k2, v2, kv_head):
    low = lax.broadcasted_iota(jnp.int32, k2.shape, 1) < HEAD_DIM
    k2r = pltpu.roll(k2, HEAD_DIM, axis=1)
    v2r = pltpu.roll(v2, HEAD_DIM, axis=1)
    if kv_head == 0:
        keys, vals = jnp.where(low, k2, k2r), jnp.where(low, v2, v2r)
    else:
        keys, vals = jnp.where(low, k2r, k2), jnp.where(low, v2r, v2)
    vals = jnp.concatenate([vals, jnp.ones_like(vals)], axis=1)
    return keys.astype(BF16), vals.astype(BF16)


def _kv_attention(q_pieces, sink_pieces, keys, vals, bias):
    n, rows = len(q_pieces), q_pieces[0].shape[0]
    q = jnp.concatenate(q_pieces, axis=0) if n > 1 else q_pieces[0]
    q_low = lax.broadcasted_iota(jnp.int32, q.shape, 1) < HEAD_DIM
    zero = jnp.zeros_like(q)
    q_all = jnp.concatenate([jnp.where(q_low, q, zero), jnp.where(q_low, zero, q)], axis=0)
    s_all = lax.dot_general(q_all, keys, (((1,), (1,)), ((), ())), preferred_element_type=F32)
    probs, sink_terms = [], []
    for parity in range(2):
        for piece in range(n):
            lo = (parity * n + piece) * rows
            s = s_all[lo:lo + rows] + bias
            m = jnp.max(s, axis=-1, keepdims=True)
            probs.append(jnp.exp2(s - m).astype(BF16))
            sink_terms.append(jnp.exp2(sink_pieces[parity][piece] - m))
    o_all = jnp.dot(jnp.concatenate(probs, axis=0), vals, preferred_element_type=F32)
    o_low = lax.broadcasted_iota(jnp.int32, (rows, LANES), 1) < HEAD_DIM
    outs = []
    for piece in range(n):
        even = o_all[piece * rows:(piece + 1) * rows]
        odd = o_all[(n + piece) * rows:(n + piece + 1) * rows]
        num = jnp.where(o_low, even[:, :LANES], odd[:, :LANES])
        den = jnp.where(o_low, even[:, LANES:] + sink_terms[piece], odd[:, LANES:] + sink_terms[n + piece])
        outs.append(num / den)
    return outs


def _attn_prompt_kernel(sink_ref, bias_ref, q_ref, kp_ref, ko_ref, vp_ref, vo_ref, o_ref):
    slabs = GROUP // 2
    for blk in range(q_ref.shape[0] // WINDOW):
        rows = slice(blk * WINDOW, (blk + 1) * WINDOW)
        if blk == 0:
            bias = bias_ref[jnp.where(pl.program_id(1) > 0, 0, 1)]
            k_prev, v_prev = kp_ref[...], vp_ref[...]
        else:
            bias = bias_ref[0]
            k_prev, v_prev = ko_ref[(blk - 1) * WINDOW:blk * WINDOW, :], vo_ref[(blk - 1) * WINDOW:blk * WINDOW, :]
        k2 = jnp.concatenate([k_prev, ko_ref[rows, :]], axis=0)
        v2 = jnp.concatenate([v_prev, vo_ref[rows, :]], axis=0)
        for kh in range(N_KV_HEADS):
            keys, vals = _kv_tiles(k2, v2, kh)
            cols = [slice((kh * slabs + s) * LANES, (kh * slabs + s + 1) * LANES) for s in range(slabs)]
            sinks = [[sink_ref[kh * GROUP + 2 * s + parity] * LOG2E for s in range(slabs)] for parity in range(2)]
            outs = _kv_attention([q_ref[rows, c] for c in cols], sinks, keys, vals, bias)
            for c, o in zip(cols, outs):
                o_ref[rows, c] = o.astype(o_ref.dtype)


def _band_bias(query_pos, first_block):
    i = np.asarray(query_pos)[:, None]
    j = np.arange(2 * WINDOW)[None, :]
    prev_ok = (j < WINDOW) & (j > i) & (not first_block)
    own_ok = (j >= WINDOW) & (j - WINDOW <= i)
    return np.where(prev_ok | own_ok, 0.0, NEG_INF).astype(np.float32)


def _attn_prompt(sinks, q, k, v, *, batch, seq, tq):
    t = q.shape[0]
    steps = seq // tq
    per_step = tq // WINDOW
    bias = jnp.asarray(np.stack([_band_bias(np.arange(WINDOW), False), _band_bias(np.arange(WINDOW), True)]))
    own = lambda b, i: (b * steps + i, 0)
    prev = lambda b, i: ((b * steps + i) * per_step - jnp.minimum(i, 1), 0)
    return pl.pallas_call(
        _attn_prompt_kernel,
        grid=(batch, steps),
        in_specs=[pl.BlockSpec(memory_space=pltpu.SMEM), _resident(bias.shape),
                  pl.BlockSpec((tq, Q_WIDTH), own),
                  pl.BlockSpec((WINDOW, KV_WIDTH), prev), pl.BlockSpec((tq, KV_WIDTH), own),
                  pl.BlockSpec((WINDOW, KV_WIDTH), prev), pl.BlockSpec((tq, KV_WIDTH), own)],
        out_specs=pl.BlockSpec((tq, Q_WIDTH), own),
        out_shape=jax.ShapeDtypeStruct((t, Q_WIDTH), BF16),
        compiler_params=pltpu.CompilerParams(
            dimension_semantics=("arbitrary", "arbitrary"), vmem_limit_bytes=VMEM_LIMIT_BYTES),
        name="attn_prompt",
    )(sinks, bias, q, k, k, v, v)


def _attn_sample_kernel(sink_ref, bias_ref, q_ref, kn_ref, vn_ref, kc_ref, vc_ref, o_ref, wk_ref, wv_ref, *,
                        dec_seq):
    rows = (GROUP // 2) * dec_seq
    key_pad = jnp.zeros((WINDOW - SAMPLE_KEY_PAD, KV_WIDTH), F32)
    bias = bias_ref[...]
    slab = lax.broadcasted_iota(jnp.int32, (rows, 1), 0) // dec_seq
    sinks = []
    for kh in range(N_KV_HEADS):
        per_parity = []
        for parity in range(2):
            sink = jnp.zeros((rows, 1), F32)
            for s in range(GROUP // 2):
                sink = jnp.where(slab == s, sink_ref[kh * GROUP + 2 * s + parity] * LOG2E, sink)
            per_parity.append([sink])
        sinks.append(per_parity)
    for b in range(q_ref.shape[0]):
        kc, vc, kn, vn = kc_ref[b], vc_ref[b], kn_ref[b], vn_ref[b]
        wk_ref[b, 0:WINDOW - dec_seq, :] = kc[dec_seq:, :]
        wk_ref[b, WINDOW - dec_seq:WINDOW, :] = kn[0:dec_seq, :]
        wv_ref[b, 0:WINDOW - dec_seq, :] = vc[dec_seq:, :]
        wv_ref[b, WINDOW - dec_seq:WINDOW, :] = vn[0:dec_seq, :]
        k2 = jnp.concatenate([kc, kn, key_pad], axis=0)
        v2 = jnp.concatenate([vc, vn, key_pad], axis=0)
        for kh in range(N_KV_HEADS):
            keys, vals = _kv_tiles(k2, v2, kh)
            q_rows = q_ref[b, kh * rows:(kh + 1) * rows, :]
            (out,) = _kv_attention([q_rows], sinks[kh], keys, vals, bias)
            o_ref[b, kh * rows:(kh + 1) * rows, :] = out.astype(o_ref.dtype)


def _attn_sample(sinks, q, k_new, v_new, k_cache, v_cache, *, dec_seq):
    nb = q.shape[0]
    bt = SAMPLE_BATCH_TILE
    blk = lambda rows: pl.BlockSpec((bt, rows, LANES), lambda i: (i, 0, 0))
    q_rows = N_KV_HEADS * (GROUP // 2) * dec_seq
    bias = jnp.asarray(_band_bias(np.arange((GROUP // 2) * dec_seq) % dec_seq, False))
    return pl.pallas_call(
        functools.partial(_attn_sample_kernel, dec_seq=dec_seq),
        grid=(nb // bt,),
        in_specs=[pl.BlockSpec(memory_space=pltpu.SMEM), _resident(bias.shape), blk(q_rows),
                  blk(SAMPLE_KEY_PAD), blk(SAMPLE_KEY_PAD), blk(WINDOW), blk(WINDOW)],
        out_specs=[blk(q_rows), blk(WINDOW), blk(WINDOW)],
        out_shape=[jax.ShapeDtypeStruct((nb, q_rows, LANES), BF16),
                   jax.ShapeDtypeStruct((nb, WINDOW, KV_WIDTH), F32),
                   jax.ShapeDtypeStruct((nb, WINDOW, KV_WIDTH), F32)],
        compiler_params=pltpu.CompilerParams(
            dimension_semantics=("arbitrary",), vmem_limit_bytes=VMEM_LIMIT_BYTES),
        name="attn_sample",
    )(sinks, bias, q, k_new, v_new, k_cache, v_cache)


def _merge_kernel(h_ref, a_ref, b_ref, wg_ref, wpa_ref, wpb_ref, wo_ref, g_ref, bb_ref, o_ref):
    h = h_ref[...]
    hb = h.astype(BF16)
    gate_a = jax.nn.sigmoid(jnp.dot(hb, wg_ref[:, :D_MODEL], preferred_element_type=F32))
    pa = jnp.dot(a_ref[...], wpa_ref[...], preferred_element_type=F32)
    merged = gate_a * pa
    gate_b = jax.nn.sigmoid(jnp.dot(hb, wg_ref[:, D_MODEL:], preferred_element_type=F32))
    pb = jnp.dot(b_ref[...], wpb_ref[...], preferred_element_type=F32)
    merged = (merged + gate_b * pb).astype(BF16)
    y = jnp.dot(merged, wo_ref[...], preferred_element_type=F32)
    o_ref[...] = _layernorm(DN_ALPHA * h + y, g_ref[...], bb_ref[...])


def _merge(h, a_out, b_out, w_gate, w_pa, w_pb, w_o, g, b, *, tm):
    t = h.shape[0]
    row = lambda width: pl.BlockSpec((tm, width), lambda i: (i, 0))
    return pl.pallas_call(
        _merge_kernel,
        grid=(t // tm,),
        in_specs=[row(D_MODEL), row(A_WIDTH), row(Q_WIDTH), _resident(w_gate.shape), _resident(w_pa.shape),
                  _resident(w_pb.shape), _resident(w_o.shape), _resident(g.shape), _resident(b.shape)],
        out_specs=row(D_MODEL),
        out_shape=jax.ShapeDtypeStruct((t, D_MODEL), F32),
        compiler_params=pltpu.CompilerParams(
            dimension_semantics=("arbitrary",), vmem_limit_bytes=VMEM_LIMIT_BYTES),
        name="merge",
    )(h, a_out, b_out, w_gate, w_pa, w_pb, w_o, g, b)


def _rope_tables(positions):
    half = HEAD_DIM // 2
    inv = ROPE_THETA ** (-np.arange(half, dtype=np.float64) / half)
    ang = np.asarray(positions, np.float64)[:, None] * inv[None, :]
    cos = np.tile(np.cos(ang), (1, 2 * LANES // HEAD_DIM))
    sin = np.tile(np.concatenate([-np.sin(ang), np.sin(ang)], axis=1), (1, LANES // HEAD_DIM))
    return jnp.asarray(cos, F32), jnp.asarray(sin, F32)


def kernel(x_prompt, x_sample, cache_win_k, cache_win_v, ffn1_up, ffn1_down, ln1_g, ln1_b, w_in, a_ln_g, a_ln_b,
           a_ws, a_bs, attn_sinks, w_pa, w_pb, w_o, ln2_g, ln2_b, ffn2_up, ffn2_down, ln3_g, ln3_b):
    batch, seq, _ = x_prompt.shape
    dec_batch, dec_seq, _ = x_sample.shape
    buf = cache_win_k.shape[2]
    assert DEPTH == 1 and buf == WINDOW and seq % ROW_TILE == 0 and ROW_TILE % CHUNK == 0 and seq % ATTN_Q_TILE == 0
    assert CHUNK % dec_seq == 0 and (dec_batch * dec_seq) % CHUNK == 0 and dec_batch % SAMPLE_BATCH_TILE == 0
    tp, ts = batch * seq, dec_batch * dec_seq
    tm_s = min(ROW_TILE, ts)

    up1, dn1 = ffn1_up[0].astype(BF16), ffn1_down[0].astype(BF16)
    up2, dn2 = ffn2_up[0].astype(BF16), ffn2_down[0].astype(BF16)
    w_mix, w_gate = w_in[0, :, :MIX_WIDTH].astype(BF16), w_in[0, :, MIX_WIDTH:].astype(BF16)
    wpa, wpb, wo = w_pa[0].astype(BF16), w_pb[0].astype(BF16), w_o[0].astype(BF16)
    sinks = attn_sinks[0]

    mix_p = a_ws[0]
    bias_p = jnp.repeat(a_bs[0].T, A_GROUP_DIM, axis=1)
    reps = CHUNK // dec_seq
    eye = jnp.eye(reps, dtype=F32)
    mix_s = jnp.einsum("ab,gts->gatbs", eye, a_ws[0, :, :dec_seq, :dec_seq]).reshape(A_GROUPS, CHUNK, CHUNK)
    bias_s = jnp.tile(bias_p[:dec_seq], (reps, 1))

    cos_p, sin_p = _rope_tables(np.arange(seq))
    cos_s, sin_s = _rope_tables(PAST_LEN + (np.arange(tm_s) % dec_seq))

    hp = _ffn_ln(x_prompt.reshape(tp, D_MODEL), up1, dn1, ln1_g, ln1_b, tm=ROW_TILE)
    ap, vnp, qp, kp, vp = _in_proj(hp, w_mix, a_ln_g, a_ln_b, mix_p, bias_p, cos_p, sin_p,
                                   tm=ROW_TILE, table_blocks=seq // ROW_TILE)
    bp = _attn_prompt(sinks, qp, kp, vp, batch=batch, seq=seq, tq=ATTN_Q_TILE)
    hp2 = _merge(hp, ap, bp, w_gate, wpa, wpb, wo, ln2_g, ln2_b, tm=ROW_TILE)
    yp = _ffn_ln(hp2, up2, dn2, ln3_g, ln3_b, tm=ROW_TILE)

    hs = _ffn_ln(x_sample.reshape(ts, D_MODEL), up1, dn1, ln1_g, ln1_b, tm=tm_s)
    as_, vns, qs, ks, vs = _in_proj(hs, w_mix, a_ln_g, a_ln_b, mix_s, bias_s, cos_s, sin_s,
                                    tm=tm_s, table_blocks=1)
    slabs = Q_WIDTH // LANES
    qs_t = qs.reshape(dec_batch, dec_seq, slabs, LANES).transpose(0, 2, 1, 3).reshape(dec_batch, slabs * dec_seq, LANES)
    pad = ((0, 0), (0, SAMPLE_KEY_PAD - dec_seq), (0, 0))
    ks_pad = jnp.pad(ks.reshape(dec_batch, dec_seq, KV_WIDTH), pad)
    vs_pad = jnp.pad(vs.reshape(dec_batch, dec_seq, KV_WIDTH), pad)
    bs_t, win_k_s, win_v_s = _attn_sample(
        sinks, qs_t, ks_pad, vs_pad, cache_win_k[0].reshape(dec_batch, buf, KV_WIDTH),
        cache_win_v[0].reshape(dec_batch, buf, KV_WIDTH), dec_seq=dec_seq)
    bs_ = bs_t.reshape(dec_batch, slabs, dec_seq, LANES).transpose(0, 2, 1, 3).reshape(ts, Q_WIDTH)
    hs2 = _merge(hs, as_, bs_, w_gate, wpa, wpb, wo, ln2_g, ln2_b, tm=tm_s)
    ys = _ffn_ln(hs2, up2, dn2, ln3_g, ln3_b, tm=tm_s)

    kv_shape = (DEPTH, -1, buf, N_KV_HEADS, HEAD_DIM)
    start = ((seq - 1) // CHUNK) * CHUNK
    return (yp.reshape(batch, seq, D_MODEL), ys.reshape(dec_batch, dec_seq, D_MODEL),
            kp.reshape(batch, seq, KV_WIDTH)[:, seq - buf:].reshape(kv_shape),
            vp.reshape(batch, seq, KV_WIDTH)[:, seq - buf:].reshape(kv_shape),
            win_k_s.reshape(kv_shape), win_v_s.reshape(kv_shape),
            vnp.reshape(batch, seq, A_WIDTH)[None, :, start:],
            vns.reshape(DEPTH, dec_batch, dec_seq, A_WIDTH))
```

```python
import functools

import numpy as np
import jax
import jax.numpy as jnp
from jax import lax
from jax.experimental import pallas as pl
from jax.experimental.pallas import tpu as pltpu

F32 = jnp.float32
BF16 = jnp.bfloat16

D_MODEL = 1024
DEPTH = 1
PAST_LEN = 16384
N_HEADS = 16
N_KV_HEADS = 2
HEAD_DIM = 64
GROUP = N_HEADS // N_KV_HEADS
WINDOW = 128
ROPE_THETA = 10000.0
CHUNK = 128
A_GROUPS = 4
A_GROUP_DIM = 128
A_WIDTH = A_GROUPS * A_GROUP_DIM
D_FF = 2816
LN_EPS = 1e-5
NEG_INF = -1e30
DN_ALPHA = (2.0 * DEPTH) ** 0.25
LOG2E = 1.4426950408889634
Q_WIDTH = N_HEADS * HEAD_DIM
KV_WIDTH = N_KV_HEADS * HEAD_DIM
MIX_WIDTH = 2 * A_WIDTH + Q_WIDTH + 2 * KV_WIDTH
IN_WIDTH = MIX_WIDTH + 2 * D_MODEL

LANES = 128
VMEM_LIMIT_BYTES = 56 * 1024 * 1024

ROW_TILE = 512
ROW_SUB_TILES = 1
FF_CHUNK = 256
PROJ_GROUP = 256
SAMPLE_KEY_PAD = 16
SAMPLE_BATCH_TILE = 16
ATTN_Q_TILE = 512


def _resident(shape):
    nd = len(shape)
    return pl.BlockSpec(shape, lambda *_: (0,) * nd, pipeline_mode=pl.Buffered(1))


def _sub_tiles(rows):
    size = rows // ROW_SUB_TILES
    return [slice(i * size, (i + 1) * size) for i in range(ROW_SUB_TILES)]


def _layernorm(y, g, b):
    mu = jnp.mean(y, axis=-1, keepdims=True)
    d = y - mu
    var = jnp.mean(d * d, axis=-1, keepdims=True)
    return d * lax.rsqrt(var + LN_EPS) * g + b


def _ffn_ln_kernel(x_ref, wup_ref, wdn_ref, g_ref, b_ref, o_ref):
    for rows in _sub_tiles(x_ref.shape[0]):
        x = x_ref[rows, :].astype(F32)
        xb = x.astype(BF16)
        y = None
        for c in range(D_FF // FF_CHUNK):
            lo = c * FF_CHUNK
            gate = jnp.dot(xb, wup_ref[:, lo:lo + FF_CHUNK], preferred_element_type=F32)
            up = jnp.dot(xb, wup_ref[:, D_FF + lo:D_FF + lo + FF_CHUNK], preferred_element_type=F32)
            act = (gate * jax.nn.sigmoid(gate) * up).astype(BF16)
            part = jnp.dot(act, wdn_ref[lo:lo + FF_CHUNK, :], preferred_element_type=F32)
            y = part if y is None else y + part
        r = DN_ALPHA * x + 0.5 * y
        o_ref[rows, :] = _layernorm(r, g_ref[...], b_ref[...]).astype(o_ref.dtype)


def _ffn_ln(x, w_up, w_down, g, b, *, tm):
    t = x.shape[0]
    row = pl.BlockSpec((tm, D_MODEL), lambda i: (i, 0))
    return pl.pallas_call(
        _ffn_ln_kernel,
        grid=(t // tm,),
        in_specs=[row, _resident(w_up.shape), _resident(w_down.shape),
                  _resident(g.shape), _resident(b.shape)],
        out_specs=row,
        out_shape=jax.ShapeDtypeStruct((t, D_MODEL), F32),
        compiler_params=pltpu.CompilerParams(
            dimension_semantics=("arbitrary",), vmem_limit_bytes=VMEM_LIMIT_BYTES),
        name="ffn_ln",
    )(x, w_up, w_down, g, b)


def _rope(x, cos, sin_signed, first_half):
    partner = jnp.where(first_half, pltpu.roll(x, LANES - HEAD_DIM // 2, axis=1),
                        pltpu.roll(x, HEAD_DIM // 2, axis=1))
    return x * cos + partner * sin_signed


def _in_proj_kernel(h_ref, w_ref, lng_ref, lnb_ref, mix_ref, bias_ref, rope_ref,
                    a_ref, vn_ref, q_ref, k_ref, v_ref):
    r = lax.broadcasted_iota(jnp.int32, (CHUNK, CHUNK), 0)
    c = lax.broadcasted_iota(jnp.int32, (CHUNK, CHUNK), 1)
    mixes = [jnp.where(c <= r, mix_ref[g], 0.0).astype(BF16) for g in range(A_GROUPS)]
    for rows in _sub_tiles(h_ref.shape[0]):
        hb = h_ref[rows, :].astype(BF16)
        cos_k, sin_k, cos_q, sin_q = [rope_ref[rows, i * LANES:(i + 1) * LANES] for i in range(4)]
        lane = lax.broadcasted_iota(jnp.int32, cos_k.shape, 1)
        first_half = (lane & (HEAD_DIM - 1)) < (HEAD_DIM // 2)

        def project(lo):
            return jnp.dot(hb, w_ref[:, lo:lo + PROJ_GROUP], preferred_element_type=F32)

        def rotary_q(i):
            x = project(2 * A_WIDTH + i * PROJ_GROUP)
            for j in range(PROJ_GROUP // LANES):
                cols = slice(i * PROJ_GROUP + j * LANES, i * PROJ_GROUP + (j + 1) * LANES)
                roped = _rope(x[:, j * LANES:(j + 1) * LANES], cos_q, sin_q, first_half)
                q_ref[rows, cols] = roped.astype(q_ref.dtype)

        gu, gv = [], []
        for i in range(A_WIDTH // PROJ_GROUP):
            gu.append(jax.nn.gelu(project(i * PROJ_GROUP)))
            rotary_q(2 * i)
            gv.append(jax.nn.gelu(project(A_WIDTH + i * PROJ_GROUP)))
            rotary_q(2 * i + 1)
        kv = project(2 * A_WIDTH + Q_WIDTH)
        k_ref[rows, :] = _rope(kv[:, :KV_WIDTH], cos_k, sin_k, first_half)
        v_ref[rows, :] = kv[:, KV_WIDTH:]

        gu = jnp.concatenate(gu, axis=1)
        vn = _layernorm(jnp.concatenate(gv, axis=1), lng_ref[...], lnb_ref[...])
        vn_ref[rows, :] = vn
        vnb = vn.astype(BF16)
        for g in range(A_GROUPS):
            cols = slice(g * A_GROUP_DIM, (g + 1) * A_GROUP_DIM)
            for ch in range((rows.stop - rows.start) // CHUNK):
                sub = slice(ch * CHUNK, (ch + 1) * CHUNK)
                out_rows = slice(rows.start + ch * CHUNK, rows.start + (ch + 1) * CHUNK)
                mixed = jnp.dot(mixes[g], vnb[sub, cols], preferred_element_type=F32) + bias_ref[:, cols]
                a_ref[out_rows, cols] = (gu[sub, cols] * mixed).astype(a_ref.dtype)


def _in_proj(h, w_mix, a_ln_g, a_ln_b, mix, bias_full, rope, *, tm, table_blocks):
    t = h.shape[0]
    row = lambda width: pl.BlockSpec((tm, width), lambda i: (i, 0))
    table = pl.BlockSpec((tm, 4 * LANES), lambda i: (i % table_blocks, 0))
    return pl.pallas_call(
        _in_proj_kernel,
        grid=(t // tm,),
        in_specs=[row(D_MODEL), _resident(w_mix.shape), _resident(a_ln_g.shape), _resident(a_ln_b.shape),
                  _resident(mix.shape), _resident(bias_full.shape), table],
        out_specs=[row(A_WIDTH), row(A_WIDTH), row(Q_WIDTH), row(KV_WIDTH), row(KV_WIDTH)],
        out_shape=[jax.ShapeDtypeStruct((t, A_WIDTH), BF16), jax.ShapeDtypeStruct((t, A_WIDTH), F32),
                   jax.ShapeDtypeStruct((t, Q_WIDTH), BF16), jax.ShapeDtypeStruct((t, KV_WIDTH), F32),
                   jax.ShapeDtypeStruct((t, KV_WIDTH), F32)],
        compiler_params=pltpu.CompilerParams(
            dimension_semantics=("arbitrary",), vmem_limit_bytes=VMEM_LIMIT_BYTES),
        name="in_proj",
    )(h, w_mix, a_ln_g, a_ln_b, mix, bias_full, rope)


def _kv_tiles(k2, v2, kv_head):
    low = lax.broadcasted_iota(jnp.int32, k2.shape, 1) < HEAD_DIM
    k2r = pltpu.roll(k2, HEAD_DIM, axis=1)
    v2r = pltpu.roll(v2, HEAD_DIM, axis=1)
    if kv_head == 0:
        keys, vals = jnp.where(low, k2, k2r), jnp.where(low, v2, v2r)
    else:
        keys, vals = jnp.where(low, k2r, k2), jnp.where(low, v2r, v2)
    vals = jnp.concatenate([vals, jnp.ones_like(vals)], axis=1)
    return keys.astype(BF16), vals.astype(BF16)


def _kv_attention(q_pieces, sink_pieces, keys, vals, bias):
    n, rows = len(q_pieces), q_pieces[0].shape[0]
    q = jnp.concatenate(q_pieces, axis=0) if n > 1 else q_pieces[0]
    q_low = lax.broadcasted_iota(jnp.int32, q.shape, 1) < HEAD_DIM
    zero = jnp.zeros_like(q)
    q_all = jnp.concatenate([jnp.where(q_low, q, zero), jnp.where(q_low, zero, q)], axis=0)
    s_all = lax.dot_general(q_all, keys, (((1,), (1,)), ((), ())), preferred_element_type=F32)
    probs, sink_terms = [], []
    for parity in range(2):
        for piece in range(n):
            lo = (parity * n + piece) * rows
            s = s_all[lo:lo + rows] + bias
            m = jnp.max(s, axis=-1, keepdims=True)
            probs.append(jnp.exp2(s - m).astype(BF16))
            sink_terms.append(jnp.exp2(sink_pieces[parity][piece] - m))
    o_all = jnp.dot(jnp.concatenate(probs, axis=0), vals, preferred_element_type=F32)
    o_low = lax.broadcasted_iota(jnp.int32, (rows, LANES), 1) < HEAD_DIM
    outs = []
    for piece in range(n):
        even = o_all[piece * rows:(piece + 1) * rows]
        odd = o_all[(n + piece) * rows:(n + piece + 1) * rows]
        num = jnp.where(o_low, even[:, :LANES], odd[:, :LANES])
        den = jnp.where(o_low, even[:, LANES:] + sink_terms[piece], odd[:, LANES:] + sink_terms[n + piece])
        outs.append(num / den)
    return outs


def _attn_prompt_kernel(sink_ref, bias_ref, q_ref, kp_ref, ko_ref, vp_ref, vo_ref, o_ref):
    slabs = GROUP // 2
    for blk in range(q_ref.shape[0] // WINDOW):
        rows = slice(blk * WINDOW, (blk + 1) * WINDOW)
        if blk == 0:
            bias = bias_ref[jnp.where(pl.program_id(1) > 0, 0, 1)]
            k_prev, v_prev = kp_ref[...], vp_ref[...]
        else:
            bias = bias_ref[0]
            k_prev, v_prev = ko_ref[(blk - 1) * WINDOW:blk * WINDOW, :], vo_ref[(blk - 1) * WINDOW:blk * WINDOW, :]
        k2 = jnp.concatenate([k_prev, ko_ref[rows, :]], axis=0)
        v2 = jnp.concatenate([v_prev, vo_ref[rows, :]], axis=0)
        for kh in range(N_KV_HEADS):
            keys, vals = _kv_tiles(k2, v2, kh)
            cols = [slice((kh * slabs + s) * LANES, (kh * slabs + s + 1) * LANES) for s in range(slabs)]
            sinks = [[sink_ref[kh * GROUP + 2 * s + parity] * LOG2E for s in range(slabs)] for parity in range(2)]
            outs = _kv_attention([q_ref[rows, c] for c in cols], sinks, keys, vals, bias)
            for c, o in zip(cols, outs):
                o_ref[rows, c] = o.astype(o_ref.dtype)


def _band_bias(query_pos, first_block):
    i = np.asarray(query_pos)[:, None]
    j = np.arange(2 * WINDOW)[None, :]
    prev_ok = (j < WINDOW) & (j > i) & (not first_block)
    own_ok = (j >= WINDOW) & (j - WINDOW <= i)
    return np.where(prev_ok | own_ok, 0.0, NEG_INF).astype(np.float32)


def _attn_prompt(sinks, q, k, v, *, batch, seq, tq):
    t = q.shape[0]
    steps = seq // tq
    per_step = tq // WINDOW
    bias = jnp.asarray(np.stack([_band_bias(np.arange(WINDOW), False), _band_bias(np.arange(WINDOW), True)]))
    own = lambda b, i: (b * steps + i, 0)
    prev = lambda b, i: ((b * steps + i) * per_step - jnp.minimum(i, 1), 0)
    return pl.pallas_call(
        _attn_prompt_kernel,
        grid=(batch, steps),
        in_specs=[pl.BlockSpec(memory_space=pltpu.SMEM), _resident(bias.shape),
                  pl.BlockSpec((tq, Q_WIDTH), own),
                  pl.BlockSpec((WINDOW, KV_WIDTH), prev), pl.BlockSpec((tq, KV_WIDTH), own),
                  pl.BlockSpec((WINDOW, KV_WIDTH), prev), pl.BlockSpec((tq, KV_WIDTH), own)],
        out_specs=pl.BlockSpec((tq, Q_WIDTH), own),
        out_shape=jax.ShapeDtypeStruct((t, Q_WIDTH), BF16),
        compiler_params=pltpu.CompilerParams(
            dimension_semantics=("arbitrary", "arbitrary"), vmem_limit_bytes=VMEM_LIMIT_BYTES),
        name="attn_prompt",
    )(sinks, bias, q, k, k, v, v)


def _attn_sample_kernel(sink_ref, bias_ref, q_ref, kn_ref, vn_ref, kc_ref, vc_ref, o_ref, wk_ref, wv_ref, *,
                        dec_seq):
    rows = (GROUP // 2) * dec_seq
    key_pad = jnp.zeros((WINDOW - SAMPLE_KEY_PAD, KV_WIDTH), F32)
    bias = bias_ref[...]
    slab = lax.broadcasted_iota(jnp.int32, (rows, 1), 0) // dec_seq
    sinks = []
    for kh in range(N_KV_HEADS):
        per_parity = []
        for parity in range(2):
            sink = jnp.zeros((rows, 1), F32)
            for s in range(GROUP // 2):
                sink = jnp.where(slab == s, sink_ref[kh * GROUP + 2 * s + parity] * LOG2E, sink)
            per_parity.append([sink])
        sinks.append(per_parity)
    for b in range(q_ref.shape[0]):
        kc, vc, kn, vn = kc_ref[b], vc_ref[b], kn_ref[b], vn_ref[b]
        wk_ref[b, 0:WINDOW - dec_seq, :] = kc[dec_seq:, :]
        wk_ref[b, WINDOW - dec_seq:WINDOW, :] = kn[0:dec_seq, :]
        wv_ref[b, 0:WINDOW - dec_seq, :] = vc[dec_seq:, :]
        wv_ref[b, WINDOW - dec_seq:WINDOW, :] = vn[0:dec_seq, :]
        k2 = jnp.concatenate([kc, kn, key_pad], axis=0)
        v2 = jnp.concatenate([vc, vn, key_pad], axis=0)
        for kh in range(N_KV_HEADS):
            keys, vals = _kv_tiles(k2, v2, kh)
            q_rows = q_ref[b, kh * rows:(kh + 1) * rows, :]
            (out,) = _kv_attention([q_rows], sinks[kh], keys, vals, bias)
            o_ref[b, kh * rows:(kh + 1) * rows, :] = out.astype(o_ref.dtype)


def _attn_sample(sinks, q, k_new, v_new, k_cache, v_cache, *, dec_seq):
    nb = q.shape[0]
    bt = SAMPLE_BATCH_TILE
    blk = lambda rows: pl.BlockSpec((bt, rows, LANES), lambda i: (i, 0, 0))
    q_rows = N_KV_HEADS * (GROUP // 2) * dec_seq
    bias = jnp.asarray(_band_bias(np.arange((GROUP // 2) * dec_seq) % dec_seq, False))
    return pl.pallas_call(
        functools.partial(_attn_sample_kernel, dec_seq=dec_seq),
        grid=(nb // bt,),
        in_specs=[pl.BlockSpec(memory_space=pltpu.SMEM), _resident(bias.shape), blk(q_rows),
                  blk(SAMPLE_KEY_PAD), blk(SAMPLE_KEY_PAD), blk(WINDOW), blk(WINDOW)],
        out_specs=[blk(q_rows), blk(WINDOW), blk(WINDOW)],
        out_shape=[jax.ShapeDtypeStruct((nb, q_rows, LANES), BF16),
                   jax.ShapeDtypeStruct((nb, WINDOW, KV_WIDTH), F32),
                   jax.ShapeDtypeStruct((nb, WINDOW, KV_WIDTH), F32)],
        compiler_params=pltpu.CompilerParams(
            dimension_semantics=("arbitrary",), vmem_limit_bytes=VMEM_LIMIT_BYTES),
        name="attn_sample",
    )(sinks, bias, q, k_new, v_new, k_cache, v_cache)


def _merge_kernel(h_ref, a_ref, b_ref, wg_ref, wpa_ref, wpb_ref, wo_ref, g_ref, bb_ref, o_ref):
    h = h_ref[...]
    hb = h.astype(BF16)
    a = a_ref[...]
    b = b_ref[...]
    merged = []
    for lo in range(0, D_MODEL, PROJ_GROUP):
        cols = slice(lo, lo + PROJ_GROUP)
        gate_b_cols = slice(D_MODEL + lo, D_MODEL + lo + PROJ_GROUP)
        gate_a = jax.nn.sigmoid(jnp.dot(hb, wg_ref[:, cols], preferred_element_type=F32))
        part = gate_a * jnp.dot(a, wpa_ref[:, cols], preferred_element_type=F32)
        gate_b = jax.nn.sigmoid(jnp.dot(hb, wg_ref[:, gate_b_cols], preferred_element_type=F32))
        part = part + gate_b * jnp.dot(b, wpb_ref[:, cols], preferred_element_type=F32)
        merged.append(part.astype(BF16))
    y = jnp.dot(jnp.concatenate(merged, axis=1), wo_ref[...], preferred_element_type=F32)
    o_ref[...] = _layernorm(DN_ALPHA * h + y, g_ref[...], bb_ref[...])


def _merge(h, a_out, b_out, w_gate, w_pa, w_pb, w_o, g, b, *, tm):
    t = h.shape[0]
    row = lambda width: pl.BlockSpec((tm, width), lambda i: (i, 0))
    return pl.pallas_call(
        _merge_kernel,
        grid=(t // tm,),
        in_specs=[row(D_MODEL), row(A_WIDTH), row(Q_WIDTH), _resident(w_gate.shape), _resident(w_pa.shape),
                  _resident(w_pb.shape), _resident(w_o.shape), _resident(g.shape), _resident(b.shape)],
        out_specs=row(D_MODEL),
        out_shape=jax.ShapeDtypeStruct((t, D_MODEL), F32),
        compiler_params=pltpu.CompilerParams(
            dimension_semantics=("arbitrary",), vmem_limit_bytes=VMEM_LIMIT_BYTES),
        name="merge",
    )(h, a_out, b_out, w_gate, w_pa, w_pb, w_o, g, b)


def _rope_tables(positions):
    half = HEAD_DIM // 2
    inv = ROPE_THETA ** (-np.arange(half, dtype=np.float64) / half)
    ang = np.asarray(positions, np.float64)[:, None] * inv[None, :]
    cos = np.tile(np.cos(ang), (1, 2 * LANES // HEAD_DIM))
    sin = np.tile(np.concatenate([-np.sin(ang), np.sin(ang)], axis=1), (1, LANES // HEAD_DIM))
    q_scale = LOG2E * HEAD_DIM ** -0.5
    return jnp.asarray(np.concatenate([cos, sin, cos * q_scale, sin * q_scale], axis=1), F32)


def kernel(x_prompt, x_sample, cache_win_k, cache_win_v, ffn1_up, ffn1_down, ln1_g, ln1_b, w_in, a_ln_g, a_ln_b,
           a_ws, a_bs, attn_sinks, w_pa, w_pb, w_o, ln2_g, ln2_b, ffn2_up, ffn2_down, ln3_g, ln3_b):
    batch, seq, _ = x_prompt.shape
    dec_batch, dec_seq, _ = x_sample.shape
    buf = cache_win_k.shape[2]
    assert DEPTH == 1 and buf == WINDOW and seq % ROW_TILE == 0 and ROW_TILE % CHUNK == 0 and seq % ATTN_Q_TILE == 0
    assert CHUNK % dec_seq == 0 and (dec_batch * dec_seq) % CHUNK == 0 and dec_batch % SAMPLE_BATCH_TILE == 0
    tp, ts = batch * seq, dec_batch * dec_seq
    tm_s = min(ROW_TILE, ts)

    up1, dn1 = ffn1_up[0].astype(BF16), ffn1_down[0].astype(BF16)
    up2, dn2 = ffn2_up[0].astype(BF16), ffn2_down[0].astype(BF16)
    w_mix, w_gate = w_in[0, :, :MIX_WIDTH].astype(BF16), w_in[0, :, MIX_WIDTH:].astype(BF16)
    wpa, wpb, wo = w_pa[0].astype(BF16), w_pb[0].astype(BF16), w_o[0].astype(BF16)
    sinks = attn_sinks[0]

    mix_p = a_ws[0]
    bias_p = jnp.repeat(a_bs[0].T, A_GROUP_DIM, axis=1)
    reps = CHUNK // dec_seq
    same_batch = np.arange(CHUNK)[:, None] // dec_seq == np.arange(CHUNK)[None, :] // dec_seq
    mix_s = jnp.where(same_batch, jnp.tile(a_ws[0, :, :dec_seq, :dec_seq], (1, reps, reps)), 0.0)
    bias_s = jnp.tile(bias_p[:dec_seq], (reps, 1))

    rope_p = _rope_tables(np.arange(seq))
    rope_s = _rope_tables(PAST_LEN + (np.arange(tm_s) % dec_seq))

    hp = _ffn_ln(x_prompt.reshape(tp, D_MODEL), up1, dn1, ln1_g, ln1_b, tm=ROW_TILE)
    ap, vnp, qp, kp, vp = _in_proj(hp, w_mix, a_ln_g, a_ln_b, mix_p, bias_p, rope_p,
                                   tm=ROW_TILE, table_blocks=seq // ROW_TILE)
    bp = _attn_prompt(sinks, qp, kp, vp, batch=batch, seq=seq, tq=ATTN_Q_TILE)
    hp2 = _merge(hp, ap, bp, w_gate, wpa, wpb, wo, ln2_g, ln2_b, tm=ROW_TILE)
    yp = _ffn_ln(hp2, up2, dn2, ln3_g, ln3_b, tm=ROW_TILE)

    hs = _ffn_ln(x_sample.reshape(ts, D_MODEL), up1, dn1, ln1_g, ln1_b, tm=tm_s)
    as_, vns, qs, ks, vs = _in_proj(hs, w_mix, a_ln_g, a_ln_b, mix_s, bias_s, rope_s,
                                    tm=tm_s, table_blocks=1)
    slabs = Q_WIDTH // LANES
    qs_t = qs.reshape(dec_batch, dec_seq, slabs, LANES).transpose(0, 2, 1, 3).reshape(dec_batch, slabs * dec_seq, LANES)
    pad = ((0, 0), (0, SAMPLE_KEY_PAD - dec_seq), (0, 0))
    ks_pad = jnp.pad(ks.reshape(dec_batch, dec_seq, KV_WIDTH), pad)
    vs_pad = jnp.pad(vs.reshape(dec_batch, dec_seq, KV_WIDTH), pad)
    bs_t, win_k_s, win_v_s = _attn_sample(
        sinks, qs_t, ks_pad, vs_pad, cache_win_k[0].reshape(dec_batch, buf, KV_WIDTH),
        cache_win_v[0].reshape(dec_batch, buf, KV_WIDTH), dec_seq=dec_seq)
    bs_ = bs_t.reshape(dec_batch, slabs, dec_seq, LANES).transpose(0, 2, 1, 3).reshape(ts, Q_WIDTH)
    hs2 = _merge(hs, as_, bs_, w_gate, wpa, wpb, wo, ln2_g, ln2_b, tm=tm_s)
    ys = _ffn_ln(hs2, up2, dn2, ln3_g, ln3_b, tm=tm_s)

    kv_shape = (DEPTH, -1, buf, N_KV_HEADS, HEAD_DIM)
    start = ((seq - 1) // CHUNK) * CHUNK
    return (yp.reshape(batch, seq, D_MODEL), ys.reshape(dec_batch, dec_seq, D_MODEL),
            kp.reshape(batch, seq, KV_WIDTH)[:, seq - buf:].reshape(kv_shape),
            vp.reshape(batch, seq, KV_WIDTH)[:, seq - buf:].reshape(kv_shape),
            win_k_s.reshape(kv_shape), win_v_s.reshape(kv_shape),
            vnp.reshape(batch, seq, A_WIDTH)[None, :, start:],
            vns.reshape(DEPTH, dec_batch, dec_seq, A_WIDTH))
```

```python
import functools

import numpy as np
import jax
import jax.numpy as jnp
from jax import lax
from jax.experimental import pallas as pl
from jax.experimental.pallas import tpu as pltpu

F32 = jnp.float32
BF16 = jnp.bfloat16

D_MODEL = 1024
DEPTH = 1
PAST_LEN = 16384
N_HEADS = 16
N_KV_HEADS = 2
HEAD_DIM = 64
GROUP = N_HEADS // N_KV_HEADS
WINDOW = 128
ROPE_THETA = 10000.0
CHUNK = 128
A_GROUPS = 4
A_GROUP_DIM = 128
A_WIDTH = A_GROUPS * A_GROUP_DIM
D_FF = 2816
LN_EPS = 1e-5
NEG_INF = -1e30
DN_ALPHA = (2.0 * DEPTH) ** 0.25
LOG2E = 1.4426950408889634
Q_WIDTH = N_HEADS * HEAD_DIM
KV_WIDTH = N_KV_HEADS * HEAD_DIM
MIX_WIDTH = 2 * A_WIDTH + Q_WIDTH + 2 * KV_WIDTH
IN_WIDTH = MIX_WIDTH + 2 * D_MODEL

LANES = 128
BF16_TILE_ROWS = 16
VMEM_LIMIT_BYTES = 56 * 1024 * 1024

ROW_TILE = 512
FF_CHUNK = 256
PROJ_GROUP = 256
SAMPLE_KEY_PAD = 16
SAMPLE_BATCH_TILE = 16
ATTN_Q_TILE = 512


def _resident(shape):
    nd = len(shape)
    return pl.BlockSpec(shape, lambda *_: (0,) * nd, pipeline_mode=pl.Buffered(1))


def _layernorm(y, g, b):
    mu = jnp.mean(y, axis=-1, keepdims=True)
    d = y - mu
    var = jnp.mean(d * d, axis=-1, keepdims=True)
    return d * lax.rsqrt(var + LN_EPS) * g + b


def _ffn_ln_kernel(x_ref, wup_ref, wdn_ref, g_ref, b_ref, *refs):
    n_cast = len(refs) // 2
    cast_in, o_ref, cast_out = refs[:n_cast], refs[n_cast], refs[n_cast + 1:]
    x = x_ref[...].astype(F32)
    xb = x.astype(BF16)
    y = None
    for c in range(D_FF // FF_CHUNK):
        lo = c * FF_CHUNK
        gate = jnp.dot(xb, wup_ref[:, lo:lo + FF_CHUNK], preferred_element_type=F32)
        up = jnp.dot(xb, wup_ref[:, D_FF + lo:D_FF + lo + FF_CHUNK], preferred_element_type=F32)
        act = (gate * jax.nn.sigmoid(gate) * up).astype(BF16)
        part = jnp.dot(act, wdn_ref[lo:lo + FF_CHUNK, :], preferred_element_type=F32)
        y = part if y is None else y + part
    r = DN_ALPHA * x + 0.5 * y
    o_ref[...] = _layernorm(r, g_ref[...], b_ref[...]).astype(o_ref.dtype)
    for src, dst in zip(cast_in, cast_out):
        dst[...] = src[...].astype(dst.dtype)


def _cast_chunk_rows(rows, steps):
    return next(c for c in range(BF16_TILE_ROWS, rows + 1, BF16_TILE_ROWS)
                if rows % c == 0 and rows // c <= steps)


def _ffn_ln(x, w_up, w_down, g, b, *, tm, cast=()):
    t = x.shape[0]
    steps = t // tm
    row = pl.BlockSpec((tm, D_MODEL), lambda i: (i, 0))
    cast_specs = []
    for w in cast:
        chunk = _cast_chunk_rows(w.shape[0], steps)
        last = w.shape[0] // chunk - 1
        cast_specs.append(pl.BlockSpec((chunk, w.shape[1]), lambda i, last=last: (jnp.minimum(i, last), 0)))
    out = pl.pallas_call(
        _ffn_ln_kernel,
        grid=(steps,),
        in_specs=[row, _resident(w_up.shape), _resident(w_down.shape),
                  _resident(g.shape), _resident(b.shape)] + cast_specs,
        out_specs=[row] + cast_specs,
        out_shape=[jax.ShapeDtypeStruct((t, D_MODEL), F32)] + [jax.ShapeDtypeStruct(w.shape, BF16) for w in cast],
        compiler_params=pltpu.CompilerParams(
            dimension_semantics=("arbitrary",), vmem_limit_bytes=VMEM_LIMIT_BYTES),
        name="ffn_ln",
    )(x, w_up, w_down, g, b, *cast)
    return out[0], tuple(out[1:])


def _rope(x, cos, sin_signed, first_half):
    partner = jnp.where(first_half, pltpu.roll(x, LANES - HEAD_DIM // 2, axis=1),
                        pltpu.roll(x, HEAD_DIM // 2, axis=1))
    return x * cos + partner * sin_signed


def _spatial_mix(mix_ref, bias_ref, chunk_tokens):
    r = lax.broadcasted_iota(jnp.int32, (CHUNK, CHUNK), 0)
    c = lax.broadcasted_iota(jnp.int32, (CHUNK, CHUNK), 1)
    if chunk_tokens == CHUNK:
        return [jnp.where(c <= r, mix_ref[g], 0.0).astype(BF16) for g in range(A_GROUPS)], bias_ref[...]
    token_mask = chunk_tokens - 1
    pick = jnp.where(((r & token_mask) == c) & (c < chunk_tokens), 1.0, 0.0).astype(BF16)
    keep = (c <= r) & ((c | token_mask) == (r | token_mask))
    mixes = []
    for g in range(A_GROUPS):
        rows_of_token = jnp.dot(pick, mix_ref[g].astype(BF16), preferred_element_type=F32).astype(BF16)
        tiled = lax.dot_general(rows_of_token, pick, (((1,), (1,)), ((), ())), preferred_element_type=F32)
        mixes.append(jnp.where(keep, tiled, 0.0).astype(BF16))
    token = lax.broadcasted_iota(jnp.int32, (CHUNK, A_WIDTH), 0) & token_mask
    bias = jnp.zeros((CHUNK, A_WIDTH), F32)
    for t in range(chunk_tokens):
        bias = jnp.where(token == t, bias_ref[t:t + 1, :], bias)
    return mixes, bias


def _in_proj_kernel(h_ref, w_ref, lng_ref, lnb_ref, mix_ref, bias_ref, rope_ref,
                    a_ref, vn_ref, q_ref, k_ref, v_ref, *, chunk_tokens):
    mixes, bias = _spatial_mix(mix_ref, bias_ref, chunk_tokens)
    hb = h_ref[...].astype(BF16)
    cos_k, sin_k, cos_q, sin_q = [rope_ref[:, i * LANES:(i + 1) * LANES] for i in range(4)]
    lane = lax.broadcasted_iota(jnp.int32, cos_k.shape, 1)
    first_half = (lane & (HEAD_DIM - 1)) < (HEAD_DIM // 2)

    def project(lo):
        return jnp.dot(hb, w_ref[:, lo:lo + PROJ_GROUP], preferred_element_type=F32)

    def rotary_q(i):
        x = project(2 * A_WIDTH + i * PROJ_GROUP)
        for j in range(PROJ_GROUP // LANES):
            cols = slice(i * PROJ_GROUP + j * LANES, i * PROJ_GROUP + (j + 1) * LANES)
            q_ref[:, cols] = _rope(x[:, j * LANES:(j + 1) * LANES], cos_q, sin_q, first_half).astype(q_ref.dtype)

    gu, gv = [], []
    for i in range(A_WIDTH // PROJ_GROUP):
        gu.append(jax.nn.gelu(project(i * PROJ_GROUP)))
        rotary_q(2 * i)
        gv.append(jax.nn.gelu(project(A_WIDTH + i * PROJ_GROUP)))
        rotary_q(2 * i + 1)
    kv = project(2 * A_WIDTH + Q_WIDTH)
    k_ref[...] = _rope(kv[:, :KV_WIDTH], cos_k, sin_k, first_half)
    v_ref[...] = kv[:, KV_WIDTH:]

    gu = jnp.concatenate(gu, axis=1)
    vn = _layernorm(jnp.concatenate(gv, axis=1), lng_ref[...], lnb_ref[...])
    vn_ref[...] = vn
    vnb = vn.astype(BF16)
    for g in range(A_GROUPS):
        cols = slice(g * A_GROUP_DIM, (g + 1) * A_GROUP_DIM)
        for ch in range(h_ref.shape[0] // CHUNK):
            rows = slice(ch * CHUNK, (ch + 1) * CHUNK)
            mixed = jnp.dot(mixes[g], vnb[rows, cols], preferred_element_type=F32) + bias[:, cols]
            a_ref[rows, cols] = (gu[rows, cols] * mixed).astype(a_ref.dtype)


def _in_proj(h, w_in, a_ln_g, a_ln_b, mix, bias_full, rope, *, tm, table_blocks, chunk_tokens):
    t = h.shape[0]
    row = lambda width: pl.BlockSpec((tm, width), lambda i: (i, 0))
    table = pl.BlockSpec((tm, 4 * LANES), lambda i: (i % table_blocks, 0))
    return pl.pallas_call(
        functools.partial(_in_proj_kernel, chunk_tokens=chunk_tokens),
        grid=(t // tm,),
        in_specs=[row(D_MODEL), _resident(w_in.shape), _resident(a_ln_g.shape), _resident(a_ln_b.shape),
                  _resident(mix.shape), _resident(bias_full.shape), table],
        out_specs=[row(A_WIDTH), row(A_WIDTH), row(Q_WIDTH), row(KV_WIDTH), row(KV_WIDTH)],
        out_shape=[jax.ShapeDtypeStruct((t, A_WIDTH), BF16), jax.ShapeDtypeStruct((t, A_WIDTH), F32),
                   jax.ShapeDtypeStruct((t, Q_WIDTH), BF16), jax.ShapeDtypeStruct((t, KV_WIDTH), F32),
                   jax.ShapeDtypeStruct((t, KV_WIDTH), F32)],
        compiler_params=pltpu.CompilerParams(
            dimension_semantics=("arbitrary",), vmem_limit_bytes=VMEM_LIMIT_BYTES),
        name="in_proj",
    )(h, w_in, a_ln_g, a_ln_b, mix, bias_full, rope)


def _kv_tiles(k2, v2, kv_head):
    low = lax.broadcasted_iota(jnp.int32, k2.shape, 1) < HEAD_DIM
    k2r = pltpu.roll(k2, HEAD_DIM, axis=1)
    v2r = pltpu.roll(v2, HEAD_DIM, axis=1)
    if kv_head == 0:
        keys, vals = jnp.where(low, k2, k2r), jnp.where(low, v2, v2r)
    else:
        keys, vals = jnp.where(low, k2r, k2), jnp.where(low, v2r, v2)
    vals = jnp.concatenate([vals, jnp.ones_like(vals)], axis=1)
    return keys.astype(BF16), vals.astype(BF16)


def _kv_attention(q_pieces, sink_pieces, keys, vals, bias):
    n, rows = len(q_pieces), q_pieces[0].shape[0]
    q = jnp.concatenate(q_pieces, axis=0) if n > 1 else q_pieces[0]
    q_low = lax.broadcasted_iota(jnp.int32, q.shape, 1) < HEAD_DIM
    zero = jnp.zeros_like(q)
    q_all = jnp.concatenate([jnp.where(q_low, q, zero), jnp.where(q_low, zero, q)], axis=0)
    s_all = lax.dot_general(q_all, keys, (((1,), (1,)), ((), ())), preferred_element_type=F32)
    probs, sink_terms = [], []
    for parity in range(2):
        for piece in range(n):
            lo = (parity * n + piece) * rows
            s = s_all[lo:lo + rows] + bias
            m = jnp.max(s, axis=-1, keepdims=True)
            probs.append(jnp.exp2(s - m).astype(BF16))
            sink_terms.append(jnp.exp2(sink_pieces[parity][piece] - m))
    o_all = jnp.dot(jnp.concatenate(probs, axis=0), vals, preferred_element_type=F32)
    o_low = lax.broadcasted_iota(jnp.int32, (rows, LANES), 1) < HEAD_DIM
    outs = []
    for piece in range(n):
        even = o_all[piece * rows:(piece + 1) * rows]
        odd = o_all[(n + piece) * rows:(n + piece + 1) * rows]
        num = jnp.where(o_low, even[:, :LANES], odd[:, :LANES])
        den = jnp.where(o_low, even[:, LANES:] + sink_terms[piece], odd[:, LANES:] + sink_terms[n + piece])
        outs.append(num / den)
    return outs


def _attn_prompt_kernel(sink_ref, bias_ref, q_ref, kp_ref, ko_ref, vp_ref, vo_ref, o_ref):
    slabs = GROUP // 2
    for blk in range(q_ref.shape[0] // WINDOW):
        rows = slice(blk * WINDOW, (blk + 1) * WINDOW)
        if blk == 0:
            bias = bias_ref[jnp.where(pl.program_id(1) > 0, 0, 1)]
            k_prev, v_prev = kp_ref[...], vp_ref[...]
        else:
            bias = bias_ref[0]
            k_prev, v_prev = ko_ref[(blk - 1) * WINDOW:blk * WINDOW, :], vo_ref[(blk - 1) * WINDOW:blk * WINDOW, :]
        k2 = jnp.concatenate([k_prev, ko_ref[rows, :]], axis=0)
        v2 = jnp.concatenate([v_prev, vo_ref[rows, :]], axis=0)
        for kh in range(N_KV_HEADS):
            keys, vals = _kv_tiles(k2, v2, kh)
            cols = [slice((kh * slabs + s) * LANES, (kh * slabs + s + 1) * LANES) for s in range(slabs)]
            sinks = [[sink_ref[kh * GROUP + 2 * s + parity] * LOG2E for s in range(slabs)] for parity in range(2)]
            outs = _kv_attention([q_ref[rows, c] for c in cols], sinks, keys, vals, bias)
            for c, o in zip(cols, outs):
                o_ref[rows, c] = o.astype(o_ref.dtype)


def _band_bias(query_pos, first_block):
    i = np.asarray(query_pos)[:, None]
    j = np.arange(2 * WINDOW)[None, :]
    prev_ok = (j < WINDOW) & (j > i) & (not first_block)
    own_ok = (j >= WINDOW) & (j - WINDOW <= i)
    return np.where(prev_ok | own_ok, 0.0, NEG_INF).astype(np.float32)


def _attn_prompt(sinks, q, k, v, *, batch, seq, tq):
    t = q.shape[0]
    steps = seq // tq
    per_step = tq // WINDOW
    bias = jnp.asarray(np.stack([_band_bias(np.arange(WINDOW), False), _band_bias(np.arange(WINDOW), True)]))
    own = lambda b, i: (b * steps + i, 0)
    prev = lambda b, i: ((b * steps + i) * per_step - jnp.minimum(i, 1), 0)
    return pl.pallas_call(
        _attn_prompt_kernel,
        grid=(batch, steps),
        in_specs=[pl.BlockSpec(memory_space=pltpu.SMEM), _resident(bias.shape),
                  pl.BlockSpec((tq, Q_WIDTH), own),
                  pl.BlockSpec((WINDOW, KV_WIDTH), prev), pl.BlockSpec((tq, KV_WIDTH), own),
                  pl.BlockSpec((WINDOW, KV_WIDTH), prev), pl.BlockSpec((tq, KV_WIDTH), own)],
        out_specs=pl.BlockSpec((tq, Q_WIDTH), own),
        out_shape=jax.ShapeDtypeStruct((t, Q_WIDTH), BF16),
        compiler_params=pltpu.CompilerParams(
            dimension_semantics=("arbitrary", "arbitrary"), vmem_limit_bytes=VMEM_LIMIT_BYTES),
        name="attn_prompt",
    )(sinks, bias, q, k, k, v, v)


def _attn_sample_kernel(sink_ref, bias_ref, q_ref, kn_ref, vn_ref, kc_ref, vc_ref, o_ref, wk_ref, wv_ref, *,
                        dec_seq):
    rows = (GROUP // 2) * dec_seq
    key_pad = jnp.zeros((WINDOW - SAMPLE_KEY_PAD, KV_WIDTH), F32)
    bias = bias_ref[...]
    slab = lax.broadcasted_iota(jnp.int32, (rows, 1), 0) // dec_seq
    sinks = []
    for kh in range(N_KV_HEADS):
        per_parity = []
        for parity in range(2):
            sink = jnp.zeros((rows, 1), F32)
            for s in range(GROUP // 2):
                sink = jnp.where(slab == s, sink_ref[kh * GROUP + 2 * s + parity] * LOG2E, sink)
            per_parity.append([sink])
        sinks.append(per_parity)
    for b in range(q_ref.shape[0]):
        kc, vc, kn, vn = kc_ref[b], vc_ref[b], kn_ref[b], vn_ref[b]
        wk_ref[b, 0:WINDOW - dec_seq, :] = kc[dec_seq:, :]
        wk_ref[b, WINDOW - dec_seq:WINDOW, :] = kn[0:dec_seq, :]
        wv_ref[b, 0:WINDOW - dec_seq, :] = vc[dec_seq:, :]
        wv_ref[b, WINDOW - dec_seq:WINDOW, :] = vn[0:dec_seq, :]
        k2 = jnp.concatenate([kc, kn, key_pad], axis=0)
        v2 = jnp.concatenate([vc, vn, key_pad], axis=0)
        for kh in range(N_KV_HEADS):
            keys, vals = _kv_tiles(k2, v2, kh)
            q_rows = q_ref[b, kh * rows:(kh + 1) * rows, :]
            (out,) = _kv_attention([q_rows], sinks[kh], keys, vals, bias)
            o_ref[b, kh * rows:(kh + 1) * rows, :] = out.astype(o_ref.dtype)


def _attn_sample(sinks, q, k_new, v_new, k_cache, v_cache, *, dec_seq):
    nb = q.shape[0]
    bt = SAMPLE_BATCH_TILE
    blk = lambda rows: pl.BlockSpec((bt, rows, LANES), lambda i: (i, 0, 0))
    q_rows = N_KV_HEADS * (GROUP // 2) * dec_seq
    bias = jnp.asarray(_band_bias(np.arange((GROUP // 2) * dec_seq) % dec_seq, False))
    return pl.pallas_call(
        functools.partial(_attn_sample_kernel, dec_seq=dec_seq),
        grid=(nb // bt,),
        in_specs=[pl.BlockSpec(memory_space=pltpu.SMEM), _resident(bias.shape), blk(q_rows),
                  blk(SAMPLE_KEY_PAD), blk(SAMPLE_KEY_PAD), blk(WINDOW), blk(WINDOW)],
        out_specs=[blk(q_rows), blk(WINDOW), blk(WINDOW)],
        out_shape=[jax.ShapeDtypeStruct((nb, q_rows, LANES), BF16),
                   jax.ShapeDtypeStruct((nb, WINDOW, KV_WIDTH), F32),
                   jax.ShapeDtypeStruct((nb, WINDOW, KV_WIDTH), F32)],
        compiler_params=pltpu.CompilerParams(
            dimension_semantics=("arbitrary",), vmem_limit_bytes=VMEM_LIMIT_BYTES),
        name="attn_sample",
    )(sinks, bias, q, k_new, v_new, k_cache, v_cache)


def _merge_kernel(h_ref, a_ref, b_ref, win_ref, wpa_ref, wpb_ref, wo_ref, g_ref, bb_ref, o_ref):
    h = h_ref[...]
    hb = h.astype(BF16)
    a = a_ref[...]
    b = b_ref[...]
    merged = []
    for lo in range(0, D_MODEL, PROJ_GROUP):
        cols = slice(lo, lo + PROJ_GROUP)
        gate_a_cols = slice(MIX_WIDTH + lo, MIX_WIDTH + lo + PROJ_GROUP)
        gate_b_cols = slice(MIX_WIDTH + D_MODEL + lo, MIX_WIDTH + D_MODEL + lo + PROJ_GROUP)
        gate_a = jax.nn.sigmoid(jnp.dot(hb, win_ref[:, gate_a_cols], preferred_element_type=F32))
        part = gate_a * jnp.dot(a, wpa_ref[:, cols], preferred_element_type=F32)
        gate_b = jax.nn.sigmoid(jnp.dot(hb, win_ref[:, gate_b_cols], preferred_element_type=F32))
        part = part + gate_b * jnp.dot(b, wpb_ref[:, cols], preferred_element_type=F32)
        merged.append(part.astype(BF16))
    y = jnp.dot(jnp.concatenate(merged, axis=1), wo_ref[...], preferred_element_type=F32)
    o_ref[...] = _layernorm(DN_ALPHA * h + y, g_ref[...], bb_ref[...])


def _merge(h, a_out, b_out, w_in, w_pa, w_pb, w_o, g, b, *, tm):
    t = h.shape[0]
    row = lambda width: pl.BlockSpec((tm, width), lambda i: (i, 0))
    return pl.pallas_call(
        _merge_kernel,
        grid=(t // tm,),
        in_specs=[row(D_MODEL), row(A_WIDTH), row(Q_WIDTH), _resident(w_in.shape), _resident(w_pa.shape),
                  _resident(w_pb.shape), _resident(w_o.shape), _resident(g.shape), _resident(b.shape)],
        out_specs=row(D_MODEL),
        out_shape=jax.ShapeDtypeStruct((t, D_MODEL), F32),
        compiler_params=pltpu.CompilerParams(
            dimension_semantics=("arbitrary",), vmem_limit_bytes=VMEM_LIMIT_BYTES),
        name="merge",
    )(h, a_out, b_out, w_in, w_pa, w_pb, w_o, g, b)


def _rope_tables(positions):
    half = HEAD_DIM // 2
    inv = ROPE_THETA ** (-np.arange(half, dtype=np.float64) / half)
    ang = np.asarray(positions, np.float64)[:, None] * inv[None, :]
    cos = np.tile(np.cos(ang), (1, 2 * LANES // HEAD_DIM))
    sin = np.tile(np.concatenate([-np.sin(ang), np.sin(ang)], axis=1), (1, LANES // HEAD_DIM))
    q_scale = LOG2E * HEAD_DIM ** -0.5
    return jnp.asarray(np.concatenate([cos, sin, cos * q_scale, sin * q_scale], axis=1), F32)


def kernel(x_prompt, x_sample, cache_win_k, cache_win_v, ffn1_up, ffn1_down, ln1_g, ln1_b, w_in, a_ln_g, a_ln_b,
           a_ws, a_bs, attn_sinks, w_pa, w_pb, w_o, ln2_g, ln2_b, ffn2_up, ffn2_down, ln3_g, ln3_b):
    batch, seq, _ = x_prompt.shape
    dec_batch, dec_seq, _ = x_sample.shape
    buf = cache_win_k.shape[2]
    assert DEPTH == 1 and buf == WINDOW and seq % ROW_TILE == 0 and ROW_TILE % CHUNK == 0 and seq % ATTN_Q_TILE == 0
    assert CHUNK % dec_seq == 0 and (dec_batch * dec_seq) % CHUNK == 0 and dec_batch % SAMPLE_BATCH_TILE == 0
    assert dec_seq & (dec_seq - 1) == 0
    tp, ts = batch * seq, dec_batch * dec_seq
    tm_s = min(ROW_TILE, ts)

    up1, dn1 = ffn1_up[0].astype(BF16), ffn1_down[0].astype(BF16)
    sinks = attn_sinks[0]
    mix = a_ws[0]
    mix_bias = jnp.repeat(a_bs[0].T, A_GROUP_DIM, axis=1)
    rope_p = _rope_tables(np.arange(seq))
    rope_s = _rope_tables(PAST_LEN + (np.arange(tm_s) % dec_seq))

    hp, (up2, dn2, w_in_b, wpa, wpb, wo) = _ffn_ln(
        x_prompt.reshape(tp, D_MODEL), up1, dn1, ln1_g, ln1_b, tm=ROW_TILE,
        cast=(ffn2_up[0], ffn2_down[0], w_in[0], w_pa[0], w_pb[0], w_o[0]))
    ap, vnp, qp, kp, vp = _in_proj(hp, w_in_b, a_ln_g, a_ln_b, mix, mix_bias, rope_p,
                                   tm=ROW_TILE, table_blocks=seq // ROW_TILE, chunk_tokens=CHUNK)
    bp = _attn_prompt(sinks, qp, kp, vp, batch=batch, seq=seq, tq=ATTN_Q_TILE)
    hp2 = _merge(hp, ap, bp, w_in_b, wpa, wpb, wo, ln2_g, ln2_b, tm=ROW_TILE)
    yp, _ = _ffn_ln(hp2, up2, dn2, ln3_g, ln3_b, tm=ROW_TILE)

    hs, _ = _ffn_ln(x_sample.reshape(ts, D_MODEL), up1, dn1, ln1_g, ln1_b, tm=tm_s)
    as_, vns, qs, ks, vs = _in_proj(hs, w_in_b, a_ln_g, a_ln_b, mix, mix_bias, rope_s,
                                    tm=tm_s, table_blocks=1, chunk_tokens=dec_seq)
    slabs = Q_WIDTH // LANES
    qs_t = qs.reshape(dec_batch, dec_seq, slabs, LANES).transpose(0, 2, 1, 3).reshape(dec_batch, slabs * dec_seq, LANES)
    pad = ((0, 0), (0, SAMPLE_KEY_PAD - dec_seq), (0, 0))
    ks_pad = jnp.pad(ks.reshape(dec_batch, dec_seq, KV_WIDTH), pad)
    vs_pad = jnp.pad(vs.reshape(dec_batch, dec_seq, KV_WIDTH), pad)
    bs_t, win_k_s, win_v_s = _attn_sample(
        sinks, qs_t, ks_pad, vs_pad, cache_win_k[0].reshape(dec_batch, buf, KV_WIDTH),
        cache_win_v[0].reshape(dec_batch, buf, KV_WIDTH), dec_seq=dec_seq)
    bs_ = bs_t.reshape(dec_batch, slabs, dec_seq, LANES).transpose(0, 2, 1, 3).reshape(ts, Q_WIDTH)
    hs2 = _merge(hs, as_, bs_, w_in_b, wpa, wpb, wo, ln2_g, ln2_b, tm=tm_s)
    ys, _ = _ffn_ln(hs2, up2, dn2, ln3_g, ln3_b, tm=tm_s)

    kv_shape = (DEPTH, -1, buf, N_KV_HEADS, HEAD_DIM)
    start = ((seq - 1) // CHUNK) * CHUNK
    return (yp.reshape(batch, seq, D_MODEL), ys.reshape(dec_batch, dec_seq, D_MODEL),
            kp.reshape(batch, seq, KV_WIDTH)[:, seq - buf:].reshape(kv_shape),
            vp.reshape(batch, seq, KV_WIDTH)[:, seq - buf:].reshape(kv_shape),
            win_k_s.reshape(kv_shape), win_v_s.reshape(kv_shape),
            vnp.reshape(batch, seq, A_WIDTH)[None, :, start:],
            vns.reshape(DEPTH, dec_batch, dec_seq, A_WIDTH))
```

```python
import functools

import numpy as np
import jax
import jax.numpy as jnp
from jax import lax
from jax.experimental import pallas as pl
from jax.experimental.pallas import tpu as pltpu

F32 = jnp.float32
BF16 = jnp.bfloat16

D_MODEL = 1024
DEPTH = 1
PAST_LEN = 16384
N_HEADS = 16
N_KV_HEADS = 2
HEAD_DIM = 64
GROUP = N_HEADS // N_KV_HEADS
WINDOW = 128
ROPE_THETA = 10000.0
CHUNK = 128
A_GROUPS = 4
A_GROUP_DIM = 128
A_WIDTH = A_GROUPS * A_GROUP_DIM
D_FF = 2816
LN_EPS = 1e-5
NEG_INF = -1e30
DN_ALPHA = (2.0 * DEPTH) ** 0.25
LOG2E = 1.4426950408889634
Q_WIDTH = N_HEADS * HEAD_DIM
KV_WIDTH = N_KV_HEADS * HEAD_DIM
MIX_WIDTH = 2 * A_WIDTH + Q_WIDTH + 2 * KV_WIDTH
IN_WIDTH = MIX_WIDTH + 2 * D_MODEL

LANES = 128
BF16_TILE_ROWS = 16
VMEM_LIMIT_BYTES = 56 * 1024 * 1024

ROW_TILE = 512
FF_CHUNK = 256
PROJ_GROUP = 256
SAMPLE_KEY_PAD = 16
SAMPLE_BATCH_TILE = 16
ATTN_Q_TILE = 512


def _resident(shape):
    nd = len(shape)
    return pl.BlockSpec(shape, lambda *_: (0,) * nd, pipeline_mode=pl.Buffered(1))


def _row_spec(tm, width):
    return pl.BlockSpec((tm, width), lambda i: (i, 0))


def _stream_specs(tm, width):
    return (pl.BlockSpec((tm, width), lambda i: (0, 0)),
            pl.BlockSpec((tm, width), lambda i: (jnp.maximum(i - 1, 0), 0)))


def _load_rows(refs):
    if len(refs) == 1:
        return refs[0][...]
    return jnp.where(pl.program_id(0) == 0, refs[0][...], refs[1][...])


def _store_rows(refs, value):
    if len(refs) == 1:
        refs[0][...] = value.astype(refs[0].dtype)
        return
    sample_ref, prompt_ref = refs
    prompt_ref[...] = value.astype(prompt_ref.dtype)

    @pl.when(pl.program_id(0) == 0)
    def _():
        sample_ref[...] = prompt_ref[...]


def _layernorm(y, g, b):
    mu = jnp.mean(y, axis=-1, keepdims=True)
    d = y - mu
    var = jnp.mean(d * d, axis=-1, keepdims=True)
    return d * lax.rsqrt(var + LN_EPS) * g + b


def _ffn_ln_kernel(*refs, n_in, n_out):
    x_refs, (wup_ref, wdn_ref, g_ref, b_ref), rest = refs[:n_in], refs[n_in:n_in + 4], refs[n_in + 4:]
    n_cast = (len(rest) - n_out) // 2
    cast_in, o_refs, cast_out = rest[:n_cast], rest[n_cast:n_cast + n_out], rest[n_cast + n_out:]
    x = _load_rows(x_refs).astype(F32)
    xb = x.astype(BF16)
    y = None
    for c in range(D_FF // FF_CHUNK):
        lo = c * FF_CHUNK
        gate = jnp.dot(xb, wup_ref[:, lo:lo + FF_CHUNK], preferred_element_type=F32)
        up = jnp.dot(xb, wup_ref[:, D_FF + lo:D_FF + lo + FF_CHUNK], preferred_element_type=F32)
        act = (gate * jax.nn.sigmoid(gate) * up).astype(BF16)
        part = jnp.dot(act, wdn_ref[lo:lo + FF_CHUNK, :], preferred_element_type=F32)
        y = part if y is None else y + part
    r = DN_ALPHA * x + 0.5 * y
    _store_rows(o_refs, _layernorm(r, g_ref[...], b_ref[...]))
    for src, dst in zip(cast_in, cast_out):
        dst[...] = src[...].astype(dst.dtype)


def _cast_chunk_rows(rows, steps):
    return next(c for c in range(BF16_TILE_ROWS, rows + 1, BF16_TILE_ROWS)
                if rows % c == 0 and rows // c <= steps)


def _ffn_ln(xs, w_up, w_down, g, b, *, tm, out_rows, cast=()):
    xs = xs if isinstance(xs, tuple) else (xs,)
    outs = out_rows if isinstance(out_rows, tuple) else (out_rows,)
    steps = sum(outs) // tm
    pick = lambda n: _stream_specs(tm, D_MODEL) if n == 2 else (_row_spec(tm, D_MODEL),)
    cast_specs = []
    for w in cast:
        chunk = _cast_chunk_rows(w.shape[0], steps)
        last = w.shape[0] // chunk - 1
        cast_specs.append(pl.BlockSpec((chunk, w.shape[1]), lambda i, last=last: (jnp.minimum(i, last), 0)))
    out = pl.pallas_call(
        functools.partial(_ffn_ln_kernel, n_in=len(xs), n_out=len(outs)),
        grid=(steps,),
        in_specs=[*pick(len(xs)), _resident(w_up.shape), _resident(w_down.shape),
                  _resident(g.shape), _resident(b.shape)] + cast_specs,
        out_specs=[*pick(len(outs))] + cast_specs,
        out_shape=[jax.ShapeDtypeStruct((r, D_MODEL), F32) for r in outs]
        + [jax.ShapeDtypeStruct(w.shape, BF16) for w in cast],
        compiler_params=pltpu.CompilerParams(
            dimension_semantics=("arbitrary",), vmem_limit_bytes=VMEM_LIMIT_BYTES),
        name="ffn_ln",
    )(*xs, w_up, w_down, g, b, *cast)
    return tuple(out[:len(outs)]), tuple(out[len(outs):])


def _rope(x, cos, sin_signed, first_half):
    partner = jnp.where(first_half, pltpu.roll(x, LANES - HEAD_DIM // 2, axis=1),
                        pltpu.roll(x, HEAD_DIM // 2, axis=1))
    return x * cos + partner * sin_signed


def _spatial_mix(mix_ref, bias_ref, chunk_tokens):
    r = lax.broadcasted_iota(jnp.int32, (CHUNK, CHUNK), 0)
    c = lax.broadcasted_iota(jnp.int32, (CHUNK, CHUNK), 1)
    if chunk_tokens == CHUNK:
        return tuple(jnp.where(c <= r, mix_ref[g], 0.0).astype(BF16) for g in range(A_GROUPS)), bias_ref[...]
    token_mask = chunk_tokens - 1
    pick = jnp.where(((r & token_mask) == c) & (c < chunk_tokens), 1.0, 0.0).astype(BF16)
    keep = (c <= r) & ((c | token_mask) == (r | token_mask))
    mixes = []
    for g in range(A_GROUPS):
        rows_of_token = jnp.dot(pick, mix_ref[g].astype(BF16), preferred_element_type=F32).astype(BF16)
        tiled = lax.dot_general(rows_of_token, pick, (((1,), (1,)), ((), ())), preferred_element_type=F32)
        mixes.append(jnp.where(keep, tiled, 0.0).astype(BF16))
    token = lax.broadcasted_iota(jnp.int32, (CHUNK, A_WIDTH), 0) & token_mask
    bias = jnp.zeros((CHUNK, A_WIDTH), F32)
    for t in range(chunk_tokens):
        bias = jnp.where(token == t, bias_ref[t:t + 1, :], bias)
    return tuple(mixes), bias


def _in_proj_kernel(h_ref, w_ref, lng_ref, lnb_ref, mix_ref, bias_ref, rope_ref,
                    a_ref, vn_ref, q_ref, k_ref, v_ref, *, sample_tokens):
    mixes, bias = lax.cond(pl.program_id(0) == 0,
                           lambda: _spatial_mix(mix_ref, bias_ref, sample_tokens),
                           lambda: _spatial_mix(mix_ref, bias_ref, CHUNK))
    hb = h_ref[...].astype(BF16)
    cos_k, sin_k, cos_q, sin_q = [rope_ref[:, i * LANES:(i + 1) * LANES] for i in range(4)]
    lane = lax.broadcasted_iota(jnp.int32, cos_k.shape, 1)
    first_half = (lane & (HEAD_DIM - 1)) < (HEAD_DIM // 2)

    def project(lo):
        return jnp.dot(hb, w_ref[:, lo:lo + PROJ_GROUP], preferred_element_type=F32)

    def rotary_q(i):
        x = project(2 * A_WIDTH + i * PROJ_GROUP)
        for j in range(PROJ_GROUP // LANES):
            cols = slice(i * PROJ_GROUP + j * LANES, i * PROJ_GROUP + (j + 1) * LANES)
            q_ref[:, cols] = _rope(x[:, j * LANES:(j + 1) * LANES], cos_q, sin_q, first_half).astype(q_ref.dtype)

    gu, gv = [], []
    for i in range(A_WIDTH // PROJ_GROUP):
        gu.append(jax.nn.gelu(project(i * PROJ_GROUP)))
        rotary_q(2 * i)
        gv.append(jax.nn.gelu(project(A_WIDTH + i * PROJ_GROUP)))
        rotary_q(2 * i + 1)
    kv = project(2 * A_WIDTH + Q_WIDTH)
    k_ref[...] = _rope(kv[:, :KV_WIDTH], cos_k, sin_k, first_half)
    v_ref[...] = kv[:, KV_WIDTH:]

    gu = jnp.concatenate(gu, axis=1)
    vn = _layernorm(jnp.concatenate(gv, axis=1), lng_ref[...], lnb_ref[...])
    vn_ref[...] = vn
    vnb = vn.astype(BF16)
    for g in range(A_GROUPS):
        cols = slice(g * A_GROUP_DIM, (g + 1) * A_GROUP_DIM)
        for ch in range(h_ref.shape[0] // CHUNK):
            rows = slice(ch * CHUNK, (ch + 1) * CHUNK)
            mixed = jnp.dot(mixes[g], vnb[rows, cols], preferred_element_type=F32) + bias[:, cols]
            a_ref[rows, cols] = (gu[rows, cols] * mixed).astype(a_ref.dtype)


def _in_proj(h, w_in, a_ln_g, a_ln_b, mix, bias_full, rope, *, tm, table_blocks, sample_tokens):
    t = h.shape[0]
    table = pl.BlockSpec((tm, 4 * LANES), lambda i: (jnp.where(i == 0, 0, 1 + (i - 1) % table_blocks), 0))
    return pl.pallas_call(
        functools.partial(_in_proj_kernel, sample_tokens=sample_tokens),
        grid=(t // tm,),
        in_specs=[_row_spec(tm, D_MODEL), _resident(w_in.shape), _resident(a_ln_g.shape), _resident(a_ln_b.shape),
                  _resident(mix.shape), _resident(bias_full.shape), table],
        out_specs=[_row_spec(tm, A_WIDTH), _row_spec(tm, A_WIDTH), _row_spec(tm, Q_WIDTH),
                   _row_spec(tm, KV_WIDTH), _row_spec(tm, KV_WIDTH)],
        out_shape=[jax.ShapeDtypeStruct((t, A_WIDTH), BF16), jax.ShapeDtypeStruct((t, A_WIDTH), F32),
                   jax.ShapeDtypeStruct((t, Q_WIDTH), BF16), jax.ShapeDtypeStruct((t, KV_WIDTH), F32),
                   jax.ShapeDtypeStruct((t, KV_WIDTH), F32)],
        compiler_params=pltpu.CompilerParams(
            dimension_semantics=("arbitrary",), vmem_limit_bytes=VMEM_LIMIT_BYTES),
        name="in_proj",
    )(h, w_in, a_ln_g, a_ln_b, mix, bias_full, rope)


def _kv_tiles(k2, v2, kv_head):
    low = lax.broadcasted_iota(jnp.int32, k2.shape, 1) < HEAD_DIM
    k2r = pltpu.roll(k2, HEAD_DIM, axis=1)
    v2r = pltpu.roll(v2, HEAD_DIM, axis=1)
    if kv_head == 0:
        keys, vals = jnp.where(low, k2, k2r), jnp.where(low, v2, v2r)
    else:
        keys, vals = jnp.where(low, k2r, k2), jnp.where(low, v2r, v2)
    vals = jnp.concatenate([vals, jnp.ones_like(vals)], axis=1)
    return keys.astype(BF16), vals.astype(BF16)


def _kv_attention(q_pieces, sink_pieces, keys, vals, bias):
    n, rows = len(q_pieces), q_pieces[0].shape[0]
    q = jnp.concatenate(q_pieces, axis=0) if n > 1 else q_pieces[0]
    q_low = lax.broadcasted_iota(jnp.int32, q.shape, 1) < HEAD_DIM
    zero = jnp.zeros_like(q)
    q_all = jnp.concatenate([jnp.where(q_low, q, zero), jnp.where(q_low, zero, q)], axis=0)
    s_all = lax.dot_general(q_all, keys, (((1,), (1,)), ((), ())), preferred_element_type=F32)
    probs, sink_terms = [], []
    for parity in range(2):
        for piece in range(n):
            lo = (parity * n + piece) * rows
            s = s_all[lo:lo + rows] + bias
            m = jnp.max(s, axis=-1, keepdims=True)
            probs.append(jnp.exp2(s - m).astype(BF16))
            sink_terms.append(jnp.exp2(sink_pieces[parity][piece] - m))
    o_all = jnp.dot(jnp.concatenate(probs, axis=0), vals, preferred_element_type=F32)
    o_low = lax.broadcasted_iota(jnp.int32, (rows, LANES), 1) < HEAD_DIM
    outs = []
    for piece in range(n):
        even = o_all[piece * rows:(piece + 1) * rows]
        odd = o_all[(n + piece) * rows:(n + piece + 1) * rows]
        num = jnp.where(o_low, even[:, :LANES], odd[:, :LANES])
        den = jnp.where(o_low, even[:, LANES:] + sink_terms[piece], odd[:, LANES:] + sink_terms[n + piece])
        outs.append(num / den)
    return outs


def _attn_prompt_kernel(sink_ref, bias_ref, q_ref, kp_ref, ko_ref, vp_ref, vo_ref, o_ref):
    slabs = GROUP // 2
    for blk in range(q_ref.shape[0] // WINDOW):
        rows = slice(blk * WINDOW, (blk + 1) * WINDOW)
        if blk == 0:
            bias = bias_ref[jnp.where(pl.program_id(1) > 0, 0, 1)]
            k_prev, v_prev = kp_ref[...], vp_ref[...]
        else:
            bias = bias_ref[0]
            k_prev, v_prev = ko_ref[(blk - 1) * WINDOW:blk * WINDOW, :], vo_ref[(blk - 1) * WINDOW:blk * WINDOW, :]
        k2 = jnp.concatenate([k_prev, ko_ref[rows, :]], axis=0)
        v2 = jnp.concatenate([v_prev, vo_ref[rows, :]], axis=0)
        for kh in range(N_KV_HEADS):
            keys, vals = _kv_tiles(k2, v2, kh)
            cols = [slice((kh * slabs + s) * LANES, (kh * slabs + s + 1) * LANES) for s in range(slabs)]
            sinks = [[sink_ref[kh * GROUP + 2 * s + parity] * LOG2E for s in range(slabs)] for parity in range(2)]
            outs = _kv_attention([q_ref[rows, c] for c in cols], sinks, keys, vals, bias)
            for c, o in zip(cols, outs):
                o_ref[rows, c] = o.astype(o_ref.dtype)


def _band_bias(query_pos, first_block):
    i = np.asarray(query_pos)[:, None]
    j = np.arange(2 * WINDOW)[None, :]
    prev_ok = (j < WINDOW) & (j > i) & (not first_block)
    own_ok = (j >= WINDOW) & (j - WINDOW <= i)
    return np.where(prev_ok | own_ok, 0.0, NEG_INF).astype(np.float32)


def _attn_prompt(sinks, q, k, v, *, batch, seq, tq, skip_rows):
    steps = seq // tq
    per_step = tq // WINDOW
    skip = skip_rows // tq
    bias = jnp.asarray(np.stack([_band_bias(np.arange(WINDOW), False), _band_bias(np.arange(WINDOW), True)]))
    own = lambda b, i: (skip + b * steps + i, 0)
    prev = lambda b, i: ((skip + b * steps + i) * per_step - jnp.minimum(i, 1), 0)
    return pl.pallas_call(
        _attn_prompt_kernel,
        grid=(batch, steps),
        in_specs=[pl.BlockSpec(memory_space=pltpu.SMEM), _resident(bias.shape),
                  pl.BlockSpec((tq, Q_WIDTH), own),
                  pl.BlockSpec((WINDOW, KV_WIDTH), prev), pl.BlockSpec((tq, KV_WIDTH), own),
                  pl.BlockSpec((WINDOW, KV_WIDTH), prev), pl.BlockSpec((tq, KV_WIDTH), own)],
        out_specs=pl.BlockSpec((tq, Q_WIDTH), lambda b, i: (b * steps + i, 0)),
        out_shape=jax.ShapeDtypeStruct((batch * seq, Q_WIDTH), BF16),
        compiler_params=pltpu.CompilerParams(
            dimension_semantics=("arbitrary", "arbitrary"), vmem_limit_bytes=VMEM_LIMIT_BYTES),
        name="attn_prompt",
    )(sinks, bias, q, k, k, v, v)


def _attn_sample_kernel(sink_ref, bias_ref, q_ref, kn_ref, vn_ref, kc_ref, vc_ref, o_ref, wk_ref, wv_ref, *,
                        dec_seq):
    rows = (GROUP // 2) * dec_seq
    key_pad = jnp.zeros((WINDOW - SAMPLE_KEY_PAD, KV_WIDTH), F32)
    bias = bias_ref[...]
    slab = lax.broadcasted_iota(jnp.int32, (rows, 1), 0) // dec_seq
    sinks = []
    for kh in range(N_KV_HEADS):
        per_parity = []
        for parity in range(2):
            sink = jnp.zeros((rows, 1), F32)
            for s in range(GROUP // 2):
                sink = jnp.where(slab == s, sink_ref[kh * GROUP + 2 * s + parity] * LOG2E, sink)
            per_parity.append([sink])
        sinks.append(per_parity)
    for b in range(q_ref.shape[0]):
        kc, vc, kn, vn = kc_ref[b], vc_ref[b], kn_ref[b], vn_ref[b]
        wk_ref[b, 0:WINDOW - dec_seq, :] = kc[dec_seq:, :]
        wk_ref[b, WINDOW - dec_seq:WINDOW, :] = kn[0:dec_seq, :]
        wv_ref[b, 0:WINDOW - dec_seq, :] = vc[dec_seq:, :]
        wv_ref[b, WINDOW - dec_seq:WINDOW, :] = vn[0:dec_seq, :]
        k2 = jnp.concatenate([kc, kn, key_pad], axis=0)
        v2 = jnp.concatenate([vc, vn, key_pad], axis=0)
        for kh in range(N_KV_HEADS):
            keys, vals = _kv_tiles(k2, v2, kh)
            q_rows = q_ref[b, kh * rows:(kh + 1) * rows, :]
            (out,) = _kv_attention([q_rows], sinks[kh], keys, vals, bias)
            o_ref[b, kh * rows:(kh + 1) * rows, :] = out.astype(o_ref.dtype)


def _attn_sample(sinks, q, k_new, v_new, k_cache, v_cache, *, dec_seq):
    nb = q.shape[0]
    bt = SAMPLE_BATCH_TILE
    blk = lambda rows: pl.BlockSpec((bt, rows, LANES), lambda i: (i, 0, 0))
    q_rows = N_KV_HEADS * (GROUP // 2) * dec_seq
    bias = jnp.asarray(_band_bias(np.arange((GROUP // 2) * dec_seq) % dec_seq, False))
    return pl.pallas_call(
        functools.partial(_attn_sample_kernel, dec_seq=dec_seq),
        grid=(nb // bt,),
        in_specs=[pl.BlockSpec(memory_space=pltpu.SMEM), _resident(bias.shape), blk(q_rows),
                  blk(SAMPLE_KEY_PAD), blk(SAMPLE_KEY_PAD), blk(WINDOW), blk(WINDOW)],
        out_specs=[blk(q_rows), blk(WINDOW), blk(WINDOW)],
        out_shape=[jax.ShapeDtypeStruct((nb, q_rows, LANES), BF16),
                   jax.ShapeDtypeStruct((nb, WINDOW, KV_WIDTH), F32),
                   jax.ShapeDtypeStruct((nb, WINDOW, KV_WIDTH), F32)],
        compiler_params=pltpu.CompilerParams(
            dimension_semantics=("arbitrary",), vmem_limit_bytes=VMEM_LIMIT_BYTES),
        name="attn_sample",
    )(sinks, bias, q, k_new, v_new, k_cache, v_cache)


def _merge_kernel(h_ref, a_ref, bs_ref, bp_ref, win_ref, wpa_ref, wpb_ref, wo_ref, g_ref, bb_ref, o_ref):
    h = h_ref[...]
    hb = h.astype(BF16)
    a = a_ref[...]
    b = _load_rows((bs_ref, bp_ref))
    merged = []
    for lo in range(0, D_MODEL, PROJ_GROUP):
        cols = slice(lo, lo + PROJ_GROUP)
        gate_a_cols = slice(MIX_WIDTH + lo, MIX_WIDTH + lo + PROJ_GROUP)
        gate_b_cols = slice(MIX_WIDTH + D_MODEL + lo, MIX_WIDTH + D_MODEL + lo + PROJ_GROUP)
        gate_a = jax.nn.sigmoid(jnp.dot(hb, win_ref[:, gate_a_cols], preferred_element_type=F32))
        part = gate_a * jnp.dot(a, wpa_ref[:, cols], preferred_element_type=F32)
        gate_b = jax.nn.sigmoid(jnp.dot(hb, win_ref[:, gate_b_cols], preferred_element_type=F32))
        part = part + gate_b * jnp.dot(b, wpb_ref[:, cols], preferred_element_type=F32)
        merged.append(part.astype(BF16))
    y = jnp.dot(jnp.concatenate(merged, axis=1), wo_ref[...], preferred_element_type=F32)
    o_ref[...] = _layernorm(DN_ALPHA * h + y, g_ref[...], bb_ref[...])


def _merge(h, a_out, b_sample, b_prompt, w_in, w_pa, w_pb, w_o, g, b, *, tm):
    t = h.shape[0]
    return pl.pallas_call(
        _merge_kernel,
        grid=(t // tm,),
        in_specs=[_row_spec(tm, D_MODEL), _row_spec(tm, A_WIDTH), *_stream_specs(tm, Q_WIDTH), _resident(w_in.shape),
                  _resident(w_pa.shape), _resident(w_pb.shape), _resident(w_o.shape), _resident(g.shape),
                  _resident(b.shape)],
        out_specs=_row_spec(tm, D_MODEL),
        out_shape=jax.ShapeDtypeStruct((t, D_MODEL), F32),
        compiler_params=pltpu.CompilerParams(
            dimension_semantics=("arbitrary",), vmem_limit_bytes=VMEM_LIMIT_BYTES),
        name="merge",
    )(h, a_out, b_sample, b_prompt, w_in, w_pa, w_pb, w_o, g, b)


def _rope_tables(positions):
    half = HEAD_DIM // 2
    inv = ROPE_THETA ** (-np.arange(half, dtype=np.float64) / half)
    ang = np.asarray(positions, np.float64)[:, None] * inv[None, :]
    cos = np.tile(np.cos(ang), (1, 2 * LANES // HEAD_DIM))
    sin = np.tile(np.concatenate([-np.sin(ang), np.sin(ang)], axis=1), (1, LANES // HEAD_DIM))
    q_scale = LOG2E * HEAD_DIM ** -0.5
    return jnp.asarray(np.concatenate([cos, sin, cos * q_scale, sin * q_scale], axis=1), F32)


def kernel(x_prompt, x_sample, cache_win_k, cache_win_v, ffn1_up, ffn1_down, ln1_g, ln1_b, w_in, a_ln_g, a_ln_b,
           a_ws, a_bs, attn_sinks, w_pa, w_pb, w_o, ln2_g, ln2_b, ffn2_up, ffn2_down, ln3_g, ln3_b):
    batch, seq, _ = x_prompt.shape
    dec_batch, dec_seq, _ = x_sample.shape
    buf = cache_win_k.shape[2]
    assert DEPTH == 1 and buf == WINDOW and seq % ROW_TILE == 0 and ROW_TILE % CHUNK == 0 and seq % ATTN_Q_TILE == 0
    assert CHUNK % dec_seq == 0 and (dec_batch * dec_seq) % CHUNK == 0 and dec_batch % SAMPLE_BATCH_TILE == 0
    assert dec_seq & (dec_seq - 1) == 0
    tp, ts = batch * seq, dec_batch * dec_seq
    assert ts == ROW_TILE

    up1, dn1 = ffn1_up[0].astype(BF16), ffn1_down[0].astype(BF16)
    sinks = attn_sinks[0]
    mix = a_ws[0]
    mix_bias = jnp.repeat(a_bs[0].T, A_GROUP_DIM, axis=1)
    rope = _rope_tables(np.concatenate([PAST_LEN + np.arange(ts) % dec_seq, np.arange(seq)]))

    xs, xp = x_sample.reshape(ts, D_MODEL), x_prompt.reshape(tp, D_MODEL)
    (h,), (up2, dn2, w_in_b, wpa, wpb, wo) = _ffn_ln(
        (xs, xp), up1, dn1, ln1_g, ln1_b, tm=ROW_TILE, out_rows=ts + tp,
        cast=(ffn2_up[0], ffn2_down[0], w_in[0], w_pa[0], w_pb[0], w_o[0]))
    a_out, vn, q, k, v = _in_proj(h, w_in_b, a_ln_g, a_ln_b, mix, mix_bias, rope, tm=ROW_TILE,
                                  table_blocks=seq // ROW_TILE, sample_tokens=dec_seq)

    b_prompt = _attn_prompt(sinks, q, k, v, batch=batch, seq=seq, tq=ATTN_Q_TILE, skip_rows=ts)
    slabs = Q_WIDTH // LANES
    qs_t = q[:ts].reshape(dec_batch, dec_seq, slabs, LANES).transpose(0, 2, 1, 3).reshape(
        dec_batch, slabs * dec_seq, LANES)
    pad = ((0, 0), (0, SAMPLE_KEY_PAD - dec_seq), (0, 0))
    ks_pad = jnp.pad(k[:ts].reshape(dec_batch, dec_seq, KV_WIDTH), pad)
    vs_pad = jnp.pad(v[:ts].reshape(dec_batch, dec_seq, KV_WIDTH), pad)
    bs_t, win_k_s, win_v_s = _attn_sample(
        sinks, qs_t, ks_pad, vs_pad, cache_win_k[0].reshape(dec_batch, buf, KV_WIDTH),
        cache_win_v[0].reshape(dec_batch, buf, KV_WIDTH), dec_seq=dec_seq)
    b_sample = bs_t.reshape(dec_batch, slabs, dec_seq, LANES).transpose(0, 2, 1, 3).reshape(ts, Q_WIDTH)

    h2 = _merge(h, a_out, b_sample, b_prompt, w_in_b, wpa, wpb, wo, ln2_g, ln2_b, tm=ROW_TILE)
    (ys, yp), _ = _ffn_ln(h2, up2, dn2, ln3_g, ln3_b, tm=ROW_TILE, out_rows=(ts, tp))

    kp, vp, vnp = k[ts:], v[ts:], vn[ts:]


    kv_shape = (DEPTH, -1, buf, N_KV_HEADS, HEAD_DIM)
    start = ((seq - 1) // CHUNK) * CHUNK
    return (yp.reshape(batch, seq, D_MODEL), ys.reshape(dec_batch, dec_seq, D_MODEL),
            kp.reshape(batch, seq, KV_WIDTH)[:, seq - buf:].reshape(kv_shape),
            vp.reshape(batch, seq, KV_WIDTH)[:, seq - buf:].reshape(kv_shape),
            win_k_s.reshape(kv_shape), win_v_s.reshape(kv_shape),
            vnp.reshape(batch, seq, A_WIDTH)[None, :, start:],
            vn[:ts].reshape(DEPTH, dec_batch, dec_seq, A_WIDTH))
```

```python
import functools

import numpy as np
import jax
import jax.numpy as jnp
from jax import lax
from jax.experimental import pallas as pl
from jax.experimental.pallas import tpu as pltpu

F32 = jnp.float32
BF16 = jnp.bfloat16

D_MODEL = 1024
DEPTH = 1
PAST_LEN = 16384
N_HEADS = 16
N_KV_HEADS = 2
HEAD_DIM = 64
GROUP = N_HEADS // N_KV_HEADS
WINDOW = 128
ROPE_THETA = 10000.0
CHUNK = 128
A_GROUPS = 4
A_GROUP_DIM = 128
A_WIDTH = A_GROUPS * A_GROUP_DIM
D_FF = 2816
LN_EPS = 1e-5
NEG_INF = -1e30
DN_ALPHA = (2.0 * DEPTH) ** 0.25
LOG2E = 1.4426950408889634
Q_WIDTH = N_HEADS * HEAD_DIM
KV_WIDTH = N_KV_HEADS * HEAD_DIM
MIX_WIDTH = 2 * A_WIDTH + Q_WIDTH + 2 * KV_WIDTH
IN_WIDTH = MIX_WIDTH + 2 * D_MODEL

LANES = 128
BF16_TILE_ROWS = 16
VMEM_LIMIT_BYTES = 56 * 1024 * 1024

ROW_TILE = 512
FF_CHUNK = 256
PROJ_GROUP = 256
SAMPLE_KEY_PAD = 16
SAMPLE_BATCH_TILE = 16
ATTN_Q_TILE = 512


def _resident(shape):
    nd = len(shape)
    return pl.BlockSpec(shape, lambda *_: (0,) * nd, pipeline_mode=pl.Buffered(1))


def _row_spec(tm, width):
    return pl.BlockSpec((tm, width), lambda i: (i, 0))


def _stream_specs(tm, width):
    return (pl.BlockSpec((tm, width), lambda i: (0, 0)),
            pl.BlockSpec((tm, width), lambda i: (jnp.maximum(i - 1, 0), 0)))


def _load_rows(refs):
    if len(refs) == 1:
        return refs[0][...]
    return jnp.where(pl.program_id(0) == 0, refs[0][...], refs[1][...])


def _store_rows(refs, value):
    if len(refs) == 1:
        refs[0][...] = value.astype(refs[0].dtype)
        return
    sample_ref, prompt_ref = refs
    prompt_ref[...] = value.astype(prompt_ref.dtype)

    @pl.when(pl.program_id(0) == 0)
    def _():
        sample_ref[...] = prompt_ref[...]


def _layernorm(y, g, b):
    mu = jnp.mean(y, axis=-1, keepdims=True)
    d = y - mu
    var = jnp.mean(d * d, axis=-1, keepdims=True)
    return d * lax.rsqrt(var + LN_EPS) * g + b


def _ffn_ln_kernel(*refs, n_in, n_out):
    x_refs, (wup_ref, wdn_ref, g_ref, b_ref), rest = refs[:n_in], refs[n_in:n_in + 4], refs[n_in + 4:]
    n_cast = (len(rest) - n_out) // 2
    cast_in, o_refs, cast_out = rest[:n_cast], rest[n_cast:n_cast + n_out], rest[n_cast + n_out:]
    x = _load_rows(x_refs).astype(F32)
    xb = x.astype(BF16)
    y = None
    for c in range(D_FF // FF_CHUNK):
        lo = c * FF_CHUNK
        gate = jnp.dot(xb, wup_ref[:, lo:lo + FF_CHUNK], preferred_element_type=F32)
        up = jnp.dot(xb, wup_ref[:, D_FF + lo:D_FF + lo + FF_CHUNK], preferred_element_type=F32)
        act = (gate * jax.nn.sigmoid(gate) * up).astype(BF16)
        part = jnp.dot(act, wdn_ref[lo:lo + FF_CHUNK, :], preferred_element_type=F32)
        y = part if y is None else y + part
    r = DN_ALPHA * x + 0.5 * y
    _store_rows(o_refs, _layernorm(r, g_ref[...], b_ref[...]))
    for src, dst in zip(cast_in, cast_out):
        dst[...] = src[...].astype(dst.dtype)


def _cast_chunk_rows(rows, steps):
    return next(c for c in range(BF16_TILE_ROWS, rows + 1, BF16_TILE_ROWS)
                if rows % c == 0 and rows // c <= steps)


def _ffn_ln(xs, w_up, w_down, g, b, *, tm, out_rows, cast=()):
    xs = xs if isinstance(xs, tuple) else (xs,)
    outs = out_rows if isinstance(out_rows, tuple) else (out_rows,)
    steps = sum(outs) // tm
    pick = lambda n: _stream_specs(tm, D_MODEL) if n == 2 else (_row_spec(tm, D_MODEL),)
    cast_specs = []
    for w in cast:
        chunk = _cast_chunk_rows(w.shape[0], steps)
        last = w.shape[0] // chunk - 1
        cast_specs.append(pl.BlockSpec((chunk, w.shape[1]), lambda i, last=last: (jnp.minimum(i, last), 0)))
    out = pl.pallas_call(
        functools.partial(_ffn_ln_kernel, n_in=len(xs), n_out=len(outs)),
        grid=(steps,),
        in_specs=[*pick(len(xs)), _resident(w_up.shape), _resident(w_down.shape),
                  _resident(g.shape), _resident(b.shape)] + cast_specs,
        out_specs=[*pick(len(outs))] + cast_specs,
        out_shape=[jax.ShapeDtypeStruct((r, D_MODEL), F32) for r in outs]
        + [jax.ShapeDtypeStruct(w.shape, BF16) for w in cast],
        compiler_params=pltpu.CompilerParams(
            dimension_semantics=("arbitrary",), vmem_limit_bytes=VMEM_LIMIT_BYTES),
        name="ffn_ln",
    )(*xs, w_up, w_down, g, b, *cast)
    return tuple(out[:len(outs)]), tuple(out[len(outs):])


def _rope(x, cos, sin_signed, first_half):
    partner = jnp.where(first_half, pltpu.roll(x, LANES - HEAD_DIM // 2, axis=1),
                        pltpu.roll(x, HEAD_DIM // 2, axis=1))
    return x * cos + partner * sin_signed


def _spatial_mix(mix_ref, bias_ref, chunk_tokens):
    r = lax.broadcasted_iota(jnp.int32, (CHUNK, CHUNK), 0)
    c = lax.broadcasted_iota(jnp.int32, (CHUNK, CHUNK), 1)
    if chunk_tokens == CHUNK:
        return tuple(jnp.where(c <= r, mix_ref[g], 0.0).astype(BF16) for g in range(A_GROUPS)), bias_ref[...]
    token_mask = chunk_tokens - 1
    pick = jnp.where(((r & token_mask) == c) & (c < chunk_tokens), 1.0, 0.0).astype(BF16)
    keep = (c <= r) & ((c | token_mask) == (r | token_mask))
    mixes = []
    for g in range(A_GROUPS):
        rows_of_token = jnp.dot(pick, mix_ref[g].astype(BF16), preferred_element_type=F32).astype(BF16)
        tiled = lax.dot_general(rows_of_token, pick, (((1,), (1,)), ((), ())), preferred_element_type=F32)
        mixes.append(jnp.where(keep, tiled, 0.0).astype(BF16))
    token = lax.broadcasted_iota(jnp.int32, (CHUNK, A_WIDTH), 0) & token_mask
    bias = jnp.zeros((CHUNK, A_WIDTH), F32)
    for t in range(chunk_tokens):
        bias = jnp.where(token == t, bias_ref[t:t + 1, :], bias)
    return tuple(mixes), bias


def _in_proj_kernel(h_ref, w_ref, lng_ref, lnb_ref, mix_ref, bias_ref, rope_ref,
                    a_ref, q_ref, k_ref, v_ref, vn_sample_ref, vn_tail_ref, k_tail_ref, v_tail_ref, *, sample_tokens):
    mixes, bias = lax.cond(pl.program_id(0) == 0,
                           lambda: _spatial_mix(mix_ref, bias_ref, sample_tokens),
                           lambda: _spatial_mix(mix_ref, bias_ref, CHUNK))
    hb = h_ref[...].astype(BF16)
    cos_k, sin_k, cos_q, sin_q = [rope_ref[:, i * LANES:(i + 1) * LANES] for i in range(4)]
    lane = lax.broadcasted_iota(jnp.int32, cos_k.shape, 1)
    first_half = (lane & (HEAD_DIM - 1)) < (HEAD_DIM // 2)

    def project(lo):
        return jnp.dot(hb, w_ref[:, lo:lo + PROJ_GROUP], preferred_element_type=F32)

    def rotary_q(i):
        x = project(2 * A_WIDTH + i * PROJ_GROUP)
        for j in range(PROJ_GROUP // LANES):
            cols = slice(i * PROJ_GROUP + j * LANES, i * PROJ_GROUP + (j + 1) * LANES)
            q_ref[:, cols] = _rope(x[:, j * LANES:(j + 1) * LANES], cos_q, sin_q, first_half).astype(q_ref.dtype)

    gu, gv = [], []
    for i in range(A_WIDTH // PROJ_GROUP):
        gu.append(jax.nn.gelu(project(i * PROJ_GROUP)))
        rotary_q(2 * i)
        gv.append(jax.nn.gelu(project(A_WIDTH + i * PROJ_GROUP)))
        rotary_q(2 * i + 1)
    kv = project(2 * A_WIDTH + Q_WIDTH)
    tail = slice(h_ref.shape[0] - WINDOW, h_ref.shape[0])
    k_ref[...] = _rope(kv[:, :KV_WIDTH], cos_k, sin_k, first_half)
    v_ref[...] = kv[:, KV_WIDTH:]
    k_tail_ref[0] = k_ref[tail, :]
    v_tail_ref[0] = v_ref[tail, :]

    gu = jnp.concatenate(gu, axis=1)
    vn = _layernorm(jnp.concatenate(gv, axis=1), lng_ref[...], lnb_ref[...])
    vn_tail_ref[0] = vn[tail, :]

    @pl.when(pl.program_id(0) == 0)
    def _():
        vn_sample_ref[...] = vn

    vnb = vn.astype(BF16)
    for g in range(A_GROUPS):
        cols = slice(g * A_GROUP_DIM, (g + 1) * A_GROUP_DIM)
        for ch in range(h_ref.shape[0] // CHUNK):
            rows = slice(ch * CHUNK, (ch + 1) * CHUNK)
            mixed = jnp.dot(mixes[g], vnb[rows, cols], preferred_element_type=F32) + bias[:, cols]
            a_ref[rows, cols] = (gu[rows, cols] * mixed).astype(a_ref.dtype)


def _in_proj(h, w_in, a_ln_g, a_ln_b, mix, bias_full, rope, *, tm, table_blocks, sample_tokens):
    t = h.shape[0]
    sequences = (t // tm - 1) // table_blocks
    table = pl.BlockSpec((tm, 4 * LANES), lambda i: (jnp.where(i == 0, 0, 1 + (i - 1) % table_blocks), 0))
    tail = lambda width: pl.BlockSpec((1, WINDOW, width), lambda i: (jnp.maximum(i - 1, 0) // table_blocks, 0, 0))
    return pl.pallas_call(
        functools.partial(_in_proj_kernel, sample_tokens=sample_tokens),
        grid=(t // tm,),
        in_specs=[_row_spec(tm, D_MODEL), _resident(w_in.shape), _resident(a_ln_g.shape), _resident(a_ln_b.shape),
                  _resident(mix.shape), _resident(bias_full.shape), table],
        out_specs=[_row_spec(tm, A_WIDTH), _row_spec(tm, Q_WIDTH), _row_spec(tm, KV_WIDTH), _row_spec(tm, KV_WIDTH),
                   pl.BlockSpec((tm, A_WIDTH), lambda i: (0, 0)), tail(A_WIDTH), tail(KV_WIDTH), tail(KV_WIDTH)],
        out_shape=[jax.ShapeDtypeStruct((t, A_WIDTH), BF16), jax.ShapeDtypeStruct((t, Q_WIDTH), BF16),
                   jax.ShapeDtypeStruct((t, KV_WIDTH), F32), jax.ShapeDtypeStruct((t, KV_WIDTH), F32),
                   jax.ShapeDtypeStruct((tm, A_WIDTH), F32), jax.ShapeDtypeStruct((sequences, WINDOW, A_WIDTH), F32),
                   jax.ShapeDtypeStruct((sequences, WINDOW, KV_WIDTH), F32),
                   jax.ShapeDtypeStruct((sequences, WINDOW, KV_WIDTH), F32)],
        compiler_params=pltpu.CompilerParams(
            dimension_semantics=("arbitrary",), vmem_limit_bytes=VMEM_LIMIT_BYTES),
        name="in_proj",
    )(h, w_in, a_ln_g, a_ln_b, mix, bias_full, rope)


def _kv_tiles(k2, v2, kv_head):
    low = lax.broadcasted_iota(jnp.int32, k2.shape, 1) < HEAD_DIM
    k2r = pltpu.roll(k2, HEAD_DIM, axis=1)
    v2r = pltpu.roll(v2, HEAD_DIM, axis=1)
    if kv_head == 0:
        keys, vals = jnp.where(low, k2, k2r), jnp.where(low, v2, v2r)
    else:
        keys, vals = jnp.where(low, k2r, k2), jnp.where(low, v2r, v2)
    vals = jnp.concatenate([vals, jnp.ones_like(vals)], axis=1)
    return keys.astype(BF16), vals.astype(BF16)


def _kv_attention(q_pieces, sink_pieces, keys, vals, bias):
    n, rows = len(q_pieces), q_pieces[0].shape[0]
    q = jnp.concatenate(q_pieces, axis=0) if n > 1 else q_pieces[0]
    q_low = lax.broadcasted_iota(jnp.int32, q.shape, 1) < HEAD_DIM
    zero = jnp.zeros_like(q)
    q_all = jnp.concatenate([jnp.where(q_low, q, zero), jnp.where(q_low, zero, q)], axis=0)
    s_all = lax.dot_general(q_all, keys, (((1,), (1,)), ((), ())), preferred_element_type=F32)
    probs, sink_terms = [], []
    for parity in range(2):
        for piece in range(n):
            lo = (parity * n + piece) * rows
            s = s_all[lo:lo + rows] + bias
            m = jnp.max(s, axis=-1, keepdims=True)
            probs.append(jnp.exp2(s - m).astype(BF16))
            sink_terms.append(jnp.exp2(sink_pieces[parity][piece] - m))
    o_all = jnp.dot(jnp.concatenate(probs, axis=0), vals, preferred_element_type=F32)
    o_low = lax.broadcasted_iota(jnp.int32, (rows, LANES), 1) < HEAD_DIM
    outs = []
    for piece in range(n):
        even = o_all[piece * rows:(piece + 1) * rows]
        odd = o_all[(n + piece) * rows:(n + piece + 1) * rows]
        num = jnp.where(o_low, even[:, :LANES], odd[:, :LANES])
        den = jnp.where(o_low, even[:, LANES:] + sink_terms[piece], odd[:, LANES:] + sink_terms[n + piece])
        outs.append(num / den)
    return outs


def _attn_prompt_kernel(sink_ref, bias_ref, q_ref, kp_ref, ko_ref, vp_ref, vo_ref, o_ref):
    slabs = GROUP // 2
    for blk in range(q_ref.shape[0] // WINDOW):
        rows = slice(blk * WINDOW, (blk + 1) * WINDOW)
        if blk == 0:
            bias = bias_ref[jnp.where(pl.program_id(1) > 0, 0, 1)]
            k_prev, v_prev = kp_ref[...], vp_ref[...]
        else:
            bias = bias_ref[0]
            k_prev, v_prev = ko_ref[(blk - 1) * WINDOW:blk * WINDOW, :], vo_ref[(blk - 1) * WINDOW:blk * WINDOW, :]
        k2 = jnp.concatenate([k_prev, ko_ref[rows, :]], axis=0)
        v2 = jnp.concatenate([v_prev, vo_ref[rows, :]], axis=0)
        for kh in range(N_KV_HEADS):
            keys, vals = _kv_tiles(k2, v2, kh)
            cols = [slice((kh * slabs + s) * LANES, (kh * slabs + s + 1) * LANES) for s in range(slabs)]
            sinks = [[sink_ref[kh * GROUP + 2 * s + parity] * LOG2E for s in range(slabs)] for parity in range(2)]
            outs = _kv_attention([q_ref[rows, c] for c in cols], sinks, keys, vals, bias)
            for c, o in zip(cols, outs):
                o_ref[rows, c] = o.astype(o_ref.dtype)


def _band_bias(query_pos, first_block):
    i = np.asarray(query_pos)[:, None]
    j = np.arange(2 * WINDOW)[None, :]
    prev_ok = (j < WINDOW) & (j > i) & (not first_block)
    own_ok = (j >= WINDOW) & (j - WINDOW <= i)
    return np.where(prev_ok | own_ok, 0.0, NEG_INF).astype(np.float32)


def _attn_prompt(sinks, q, k, v, *, batch, seq, tq, skip_rows):
    steps = seq // tq
    per_step = tq // WINDOW
    skip = skip_rows // tq
    bias = jnp.asarray(np.stack([_band_bias(np.arange(WINDOW), False), _band_bias(np.arange(WINDOW), True)]))
    own = lambda b, i: (skip + b * steps + i, 0)
    prev = lambda b, i: ((skip + b * steps + i) * per_step - jnp.minimum(i, 1), 0)
    return pl.pallas_call(
        _attn_prompt_kernel,
        grid=(batch, steps),
        in_specs=[pl.BlockSpec(memory_space=pltpu.SMEM), _resident(bias.shape),
                  pl.BlockSpec((tq, Q_WIDTH), own),
                  pl.BlockSpec((WINDOW, KV_WIDTH), prev), pl.BlockSpec((tq, KV_WIDTH), own),
                  pl.BlockSpec((WINDOW, KV_WIDTH), prev), pl.BlockSpec((tq, KV_WIDTH), own)],
        out_specs=pl.BlockSpec((tq, Q_WIDTH), lambda b, i: (b * steps + i, 0)),
        out_shape=jax.ShapeDtypeStruct((batch * seq, Q_WIDTH), BF16),
        compiler_params=pltpu.CompilerParams(
            dimension_semantics=("arbitrary", "arbitrary"), vmem_limit_bytes=VMEM_LIMIT_BYTES),
        name="attn_prompt",
    )(sinks, bias, q, k, k, v, v)


def _attn_sample_kernel(sink_ref, bias_ref, q_ref, kn_ref, vn_ref, kc_ref, vc_ref, o_ref, wk_ref, wv_ref, *,
                        dec_seq):
    rows = (GROUP // 2) * dec_seq
    key_pad = jnp.zeros((WINDOW - SAMPLE_KEY_PAD, KV_WIDTH), F32)
    bias = bias_ref[...]
    slab = lax.broadcasted_iota(jnp.int32, (rows, 1), 0) // dec_seq
    sinks = []
    for kh in range(N_KV_HEADS):
        per_parity = []
        for parity in range(2):
            sink = jnp.zeros((rows, 1), F32)
            for s in range(GROUP // 2):
                sink = jnp.where(slab == s, sink_ref[kh * GROUP + 2 * s + parity] * LOG2E, sink)
            per_parity.append([sink])
        sinks.append(per_parity)
    for b in range(q_ref.shape[0]):
        kc, vc, kn, vn = kc_ref[b], vc_ref[b], kn_ref[b], vn_ref[b]
        wk_ref[b, 0:WINDOW - dec_seq, :] = kc[dec_seq:, :]
        wk_ref[b, WINDOW - dec_seq:WINDOW, :] = kn[0:dec_seq, :]
        wv_ref[b, 0:WINDOW - dec_seq, :] = vc[dec_seq:, :]
        wv_ref[b, WINDOW - dec_seq:WINDOW, :] = vn[0:dec_seq, :]
        k2 = jnp.concatenate([kc, kn, key_pad], axis=0)
        v2 = jnp.concatenate([vc, vn, key_pad], axis=0)
        for kh in range(N_KV_HEADS):
            keys, vals = _kv_tiles(k2, v2, kh)
            q_rows = q_ref[b, kh * rows:(kh + 1) * rows, :]
            (out,) = _kv_attention([q_rows], sinks[kh], keys, vals, bias)
            o_ref[b, kh * rows:(kh + 1) * rows, :] = out.astype(o_ref.dtype)


def _attn_sample(sinks, q, k_new, v_new, k_cache, v_cache, *, dec_seq):
    nb = q.shape[0]
    bt = SAMPLE_BATCH_TILE
    blk = lambda rows: pl.BlockSpec((bt, rows, LANES), lambda i: (i, 0, 0))
    q_rows = N_KV_HEADS * (GROUP // 2) * dec_seq
    bias = jnp.asarray(_band_bias(np.arange((GROUP // 2) * dec_seq) % dec_seq, False))
    return pl.pallas_call(
        functools.partial(_attn_sample_kernel, dec_seq=dec_seq),
        grid=(nb // bt,),
        in_specs=[pl.BlockSpec(memory_space=pltpu.SMEM), _resident(bias.shape), blk(q_rows),
                  blk(SAMPLE_KEY_PAD), blk(SAMPLE_KEY_PAD), blk(WINDOW), blk(WINDOW)],
        out_specs=[blk(q_rows), blk(WINDOW), blk(WINDOW)],
        out_shape=[jax.ShapeDtypeStruct((nb, q_rows, LANES), BF16),
                   jax.ShapeDtypeStruct((nb, WINDOW, KV_WIDTH), F32),
                   jax.ShapeDtypeStruct((nb, WINDOW, KV_WIDTH), F32)],
        compiler_params=pltpu.CompilerParams(
            dimension_semantics=("arbitrary",), vmem_limit_bytes=VMEM_LIMIT_BYTES),
        name="attn_sample",
    )(sinks, bias, q, k_new, v_new, k_cache, v_cache)


def _merge_kernel(h_ref, a_ref, bs_ref, bp_ref, win_ref, wpa_ref, wpb_ref, wo_ref, g_ref, bb_ref, o_ref):
    h = h_ref[...]
    hb = h.astype(BF16)
    a = a_ref[...]
    b = _load_rows((bs_ref, bp_ref))
    merged = []
    for lo in range(0, D_MODEL, PROJ_GROUP):
        cols = slice(lo, lo + PROJ_GROUP)
        gate_a_cols = slice(MIX_WIDTH + lo, MIX_WIDTH + lo + PROJ_GROUP)
        gate_b_cols = slice(MIX_WIDTH + D_MODEL + lo, MIX_WIDTH + D_MODEL + lo + PROJ_GROUP)
        gate_a = jax.nn.sigmoid(jnp.dot(hb, win_ref[:, gate_a_cols], preferred_element_type=F32))
        part = gate_a * jnp.dot(a, wpa_ref[:, cols], preferred_element_type=F32)
        gate_b = jax.nn.sigmoid(jnp.dot(hb, win_ref[:, gate_b_cols], preferred_element_type=F32))
        part = part + gate_b * jnp.dot(b, wpb_ref[:, cols], preferred_element_type=F32)
        merged.append(part.astype(BF16))
    y = jnp.dot(jnp.concatenate(merged, axis=1), wo_ref[...], preferred_element_type=F32)
    o_ref[...] = _layernorm(DN_ALPHA * h + y, g_ref[...], bb_ref[...])


def _merge(h, a_out, b_sample, b_prompt, w_in, w_pa, w_pb, w_o, g, b, *, tm):
    t = h.shape[0]
    return pl.pallas_call(
        _merge_kernel,
        grid=(t // tm,),
        in_specs=[_row_spec(tm, D_MODEL), _row_spec(tm, A_WIDTH), *_stream_specs(tm, Q_WIDTH), _resident(w_in.shape),
                  _resident(w_pa.shape), _resident(w_pb.shape), _resident(w_o.shape), _resident(g.shape),
                  _resident(b.shape)],
        out_specs=_row_spec(tm, D_MODEL),
        out_shape=jax.ShapeDtypeStruct((t, D_MODEL), F32),
        compiler_params=pltpu.CompilerParams(
            dimension_semantics=("arbitrary",), vmem_limit_bytes=VMEM_LIMIT_BYTES),
        name="merge",
    )(h, a_out, b_sample, b_prompt, w_in, w_pa, w_pb, w_o, g, b)


def _rope_tables(positions):
    half = HEAD_DIM // 2
    inv = ROPE_THETA ** (-np.arange(half, dtype=np.float64) / half)
    ang = np.asarray(positions, np.float64)[:, None] * inv[None, :]
    cos = np.tile(np.cos(ang), (1, 2 * LANES // HEAD_DIM))
    sin = np.tile(np.concatenate([-np.sin(ang), np.sin(ang)], axis=1), (1, LANES // HEAD_DIM))
    q_scale = LOG2E * HEAD_DIM ** -0.5
    return jnp.asarray(np.concatenate([cos, sin, cos * q_scale, sin * q_scale], axis=1), F32)


def kernel(x_prompt, x_sample, cache_win_k, cache_win_v, ffn1_up, ffn1_down, ln1_g, ln1_b, w_in, a_ln_g, a_ln_b,
           a_ws, a_bs, attn_sinks, w_pa, w_pb, w_o, ln2_g, ln2_b, ffn2_up, ffn2_down, ln3_g, ln3_b):
    batch, seq, _ = x_prompt.shape
    dec_batch, dec_seq, _ = x_sample.shape
    buf = cache_win_k.shape[2]
    assert DEPTH == 1 and buf == WINDOW and seq % ROW_TILE == 0 and ROW_TILE % CHUNK == 0 and seq % ATTN_Q_TILE == 0
    assert CHUNK % dec_seq == 0 and (dec_batch * dec_seq) % CHUNK == 0 and dec_batch % SAMPLE_BATCH_TILE == 0
    assert dec_seq & (dec_seq - 1) == 0
    tp, ts = batch * seq, dec_batch * dec_seq
    assert ts == ROW_TILE
    assert seq % CHUNK == 0 and CHUNK == WINDOW

    up1, dn1 = ffn1_up[0].astype(BF16), ffn1_down[0].astype(BF16)
    sinks = attn_sinks[0]
    mix = a_ws[0]
    mix_bias = jnp.repeat(a_bs[0].T, A_GROUP_DIM, axis=1)
    rope = _rope_tables(np.concatenate([PAST_LEN + np.arange(ts) % dec_seq, np.arange(seq)]))

    xs, xp = x_sample.reshape(ts, D_MODEL), x_prompt.reshape(tp, D_MODEL)
    (h,), (up2, dn2, w_in_b, wpa, wpb, wo) = _ffn_ln(
        (xs, xp), up1, dn1, ln1_g, ln1_b, tm=ROW_TILE, out_rows=ts + tp,
        cast=(ffn2_up[0], ffn2_down[0], w_in[0], w_pa[0], w_pb[0], w_o[0]))
    a_out, q, k, v, vn_sample, vn_tail, k_tail, v_tail = _in_proj(
        h, w_in_b, a_ln_g, a_ln_b, mix, mix_bias, rope, tm=ROW_TILE, table_blocks=seq // ROW_TILE,
        sample_tokens=dec_seq)

    b_prompt = _attn_prompt(sinks, q, k, v, batch=batch, seq=seq, tq=ATTN_Q_TILE, skip_rows=ts)
    slabs = Q_WIDTH // LANES
    qs_t = q[:ts].reshape(dec_batch, dec_seq, slabs, LANES).transpose(0, 2, 1, 3).reshape(
        dec_batch, slabs * dec_seq, LANES)
    pad = ((0, 0), (0, SAMPLE_KEY_PAD - dec_seq), (0, 0))
    ks_pad = jnp.pad(k[:ts].reshape(dec_batch, dec_seq, KV_WIDTH), pad)
    vs_pad = jnp.pad(v[:ts].reshape(dec_batch, dec_seq, KV_WIDTH), pad)
    bs_t, win_k_s, win_v_s = _attn_sample(
        sinks, qs_t, ks_pad, vs_pad, cache_win_k[0].reshape(dec_batch, buf, KV_WIDTH),
        cache_win_v[0].reshape(dec_batch, buf, KV_WIDTH), dec_seq=dec_seq)
    b_sample = bs_t.reshape(dec_batch, slabs, dec_seq, LANES).transpose(0, 2, 1, 3).reshape(ts, Q_WIDTH)

    h2 = _merge(h, a_out, b_sample, b_prompt, w_in_b, wpa, wpb, wo, ln2_g, ln2_b, tm=ROW_TILE)
    (ys, yp), _ = _ffn_ln(h2, up2, dn2, ln3_g, ln3_b, tm=ROW_TILE, out_rows=(ts, tp))

    kv_shape = (DEPTH, -1, buf, N_KV_HEADS, HEAD_DIM)
    return (yp.reshape(batch, seq, D_MODEL), ys.reshape(dec_batch, dec_seq, D_MODEL),
            k_tail.reshape(kv_shape), v_tail.reshape(kv_shape),
            win_k_s.reshape(kv_shape), win_v_s.reshape(kv_shape),
            vn_tail[None], vn_sample.reshape(DEPTH, dec_batch, dec_seq, A_WIDTH))
```

```python
import functools

import numpy as np
import jax
import jax.numpy as jnp
from jax import lax
from jax.experimental import pallas as pl
from jax.experimental.pallas import tpu as pltpu

F32 = jnp.float32
BF16 = jnp.bfloat16

D_MODEL = 1024
DEPTH = 1
PAST_LEN = 16384
N_HEADS = 16
N_KV_HEADS = 2
HEAD_DIM = 64
GROUP = N_HEADS // N_KV_HEADS
WINDOW = 128
ROPE_THETA = 10000.0
CHUNK = 128
A_GROUPS = 4
A_GROUP_DIM = 128
A_WIDTH = A_GROUPS * A_GROUP_DIM
D_FF = 2816
LN_EPS = 1e-5
NEG_INF = -1e30
DN_ALPHA = (2.0 * DEPTH) ** 0.25
LOG2E = 1.4426950408889634
Q_WIDTH = N_HEADS * HEAD_DIM
KV_WIDTH = N_KV_HEADS * HEAD_DIM
MIX_WIDTH = 2 * A_WIDTH + Q_WIDTH + 2 * KV_WIDTH
IN_WIDTH = MIX_WIDTH + 2 * D_MODEL

LANES = 128
BF16_TILE_ROWS = 16
VMEM_LIMIT_BYTES = 56 * 1024 * 1024

ROW_TILE = 512
FF_CHUNK = 256
PROJ_GROUP = 256
SAMPLE_KEY_PAD = 16
SAMPLE_BATCH_TILE = 16
ATTN_Q_TILE = 512


def _resident(shape):
    nd = len(shape)
    return pl.BlockSpec(shape, lambda *_: (0,) * nd, pipeline_mode=pl.Buffered(1))


def _row_spec(tm, width):
    return pl.BlockSpec((tm, width), lambda i: (i, 0))


def _stream_specs(tm, width):
    return (pl.BlockSpec((tm, width), lambda i: (0, 0)),
            pl.BlockSpec((tm, width), lambda i: (jnp.maximum(i - 1, 0), 0)))


def _load_rows(refs):
    if len(refs) == 1:
        return refs[0][...]
    return jnp.where(pl.program_id(0) == 0, refs[0][...], refs[1][...])


def _store_rows(refs, value):
    if len(refs) == 1:
        refs[0][...] = value.astype(refs[0].dtype)
        return
    sample_ref, prompt_ref = refs
    prompt_ref[...] = value.astype(prompt_ref.dtype)

    @pl.when(pl.program_id(0) == 0)
    def _():
        sample_ref[...] = prompt_ref[...]


def _layernorm(y, g, b):
    mu = jnp.mean(y, axis=-1, keepdims=True)
    d = y - mu
    var = jnp.mean(d * d, axis=-1, keepdims=True)
    return d * lax.rsqrt(var + LN_EPS) * g + b


def _ffn_ln_kernel(*refs, n_in, n_out):
    x_refs, (wup_ref, wdn_ref, g_ref, b_ref), rest = refs[:n_in], refs[n_in:n_in + 4], refs[n_in + 4:]
    n_cast = (len(rest) - n_out) // 2
    cast_in, o_refs, cast_out = rest[:n_cast], rest[n_cast:n_cast + n_out], rest[n_cast + n_out:]
    x = _load_rows(x_refs).astype(F32)
    xb = x.astype(BF16)
    y = None
    for c in range(D_FF // FF_CHUNK):
        lo = c * FF_CHUNK
        gate = jnp.dot(xb, wup_ref[:, lo:lo + FF_CHUNK], preferred_element_type=F32)
        up = jnp.dot(xb, wup_ref[:, D_FF + lo:D_FF + lo + FF_CHUNK], preferred_element_type=F32)
        act = (gate * jax.nn.sigmoid(gate) * up).astype(BF16)
        part = jnp.dot(act, wdn_ref[lo:lo + FF_CHUNK, :], preferred_element_type=F32)
        y = part if y is None else y + part
    r = DN_ALPHA * x + 0.5 * y
    _store_rows(o_refs, _layernorm(r, g_ref[...], b_ref[...]))
    for src, dst in zip(cast_in, cast_out):
        dst[...] = src[...].astype(dst.dtype)


def _cast_chunk_rows(rows, steps):
    return next(c for c in range(BF16_TILE_ROWS, rows + 1, BF16_TILE_ROWS)
                if rows % c == 0 and rows // c <= steps)


def _ffn_ln(xs, w_up, w_down, g, b, *, tm, out_rows, cast=()):
    xs = xs if isinstance(xs, tuple) else (xs,)
    outs = out_rows if isinstance(out_rows, tuple) else (out_rows,)
    steps = sum(outs) // tm
    pick = lambda n: _stream_specs(tm, D_MODEL) if n == 2 else (_row_spec(tm, D_MODEL),)
    cast_specs = []
    for w in cast:
        chunk = _cast_chunk_rows(w.shape[0], steps)
        last = w.shape[0] // chunk - 1
        cast_specs.append(pl.BlockSpec((chunk, w.shape[1]), lambda i, last=last: (jnp.minimum(i, last), 0)))
    out = pl.pallas_call(
        functools.partial(_ffn_ln_kernel, n_in=len(xs), n_out=len(outs)),
        grid=(steps,),
        in_specs=[*pick(len(xs)), _resident(w_up.shape), _resident(w_down.shape),
                  _resident(g.shape), _resident(b.shape)] + cast_specs,
        out_specs=[*pick(len(outs))] + cast_specs,
        out_shape=[jax.ShapeDtypeStruct((r, D_MODEL), F32) for r in outs]
        + [jax.ShapeDtypeStruct(w.shape, BF16) for w in cast],
        compiler_params=pltpu.CompilerParams(
            dimension_semantics=("arbitrary",), vmem_limit_bytes=VMEM_LIMIT_BYTES),
        name="ffn_ln",
    )(*xs, w_up, w_down, g, b, *cast)
    return tuple(out[:len(outs)]), tuple(out[len(outs):])


def _rope(x, cos, sin_signed, first_half):
    partner = jnp.where(first_half, pltpu.roll(x, LANES - HEAD_DIM // 2, axis=1),
                        pltpu.roll(x, HEAD_DIM // 2, axis=1))
    return x * cos + partner * sin_signed


def _spatial_mix(mix_ref, bias_ref, chunk_tokens):
    r = lax.broadcasted_iota(jnp.int32, (CHUNK, CHUNK), 0)
    c = lax.broadcasted_iota(jnp.int32, (CHUNK, CHUNK), 1)
    if chunk_tokens == CHUNK:
        return tuple(jnp.where(c <= r, mix_ref[g], 0.0).astype(BF16) for g in range(A_GROUPS)), bias_ref[...]
    token_mask = chunk_tokens - 1
    pick = jnp.where(((r & token_mask) == c) & (c < chunk_tokens), 1.0, 0.0).astype(BF16)
    keep = (c <= r) & ((c | token_mask) == (r | token_mask))
    mixes = []
    for g in range(A_GROUPS):
        rows_of_token = jnp.dot(pick, mix_ref[g].astype(BF16), preferred_element_type=F32).astype(BF16)
        tiled = lax.dot_general(rows_of_token, pick, (((1,), (1,)), ((), ())), preferred_element_type=F32)
        mixes.append(jnp.where(keep, tiled, 0.0).astype(BF16))
    token = lax.broadcasted_iota(jnp.int32, (CHUNK, A_WIDTH), 0) & token_mask
    bias = jnp.zeros((CHUNK, A_WIDTH), F32)
    for t in range(chunk_tokens):
        bias = jnp.where(token == t, bias_ref[t:t + 1, :], bias)
    return tuple(mixes), bias


def _in_proj_kernel(h_ref, w_ref, lng_ref, lnb_ref, mix_ref, bias_ref, rope_ref,
                    a_ref, q_ref, k_ref, v_ref, vn_first_ref, vn_tail_ref, k_tail_ref, v_tail_ref, *, sample_tokens):
    mixes, bias = lax.cond(pl.program_id(0) == 0,
                           lambda: _spatial_mix(mix_ref, bias_ref, sample_tokens),
                           lambda: _spatial_mix(mix_ref, bias_ref, CHUNK))
    hb = h_ref[...].astype(BF16)
    cos_k, sin_k, cos_q, sin_q = [rope_ref[:, i * LANES:(i + 1) * LANES] for i in range(4)]
    lane = lax.broadcasted_iota(jnp.int32, cos_k.shape, 1)
    first_half = (lane & (HEAD_DIM - 1)) < (HEAD_DIM // 2)

    def project(lo):
        return jnp.dot(hb, w_ref[:, lo:lo + PROJ_GROUP], preferred_element_type=F32)

    def rotary_q(i):
        x = project(2 * A_WIDTH + i * PROJ_GROUP)
        for j in range(PROJ_GROUP // LANES):
            cols = slice(i * PROJ_GROUP + j * LANES, i * PROJ_GROUP + (j + 1) * LANES)
            q_ref[:, cols] = _rope(x[:, j * LANES:(j + 1) * LANES], cos_q, sin_q, first_half).astype(q_ref.dtype)

    gu, gv = [], []
    for i in range(A_WIDTH // PROJ_GROUP):
        gu.append(jax.nn.gelu(project(i * PROJ_GROUP)))
        rotary_q(2 * i)
        gv.append(jax.nn.gelu(project(A_WIDTH + i * PROJ_GROUP)))
        rotary_q(2 * i + 1)
    kv = project(2 * A_WIDTH + Q_WIDTH)
    tail = slice(h_ref.shape[0] - WINDOW, h_ref.shape[0])
    k_ref[...] = _rope(kv[:, :KV_WIDTH], cos_k, sin_k, first_half)
    v_ref[...] = kv[:, KV_WIDTH:]
    k_tail_ref[0] = k_ref[tail, :]
    v_tail_ref[0] = v_ref[tail, :]

    gu = jnp.concatenate(gu, axis=1)
    vn = _layernorm(jnp.concatenate(gv, axis=1), lng_ref[...], lnb_ref[...])
    vn_tail_ref[0] = vn[tail, :]
    vn_first_ref[0] = vn
    vnb = vn.astype(BF16)
    for g in range(A_GROUPS):
        cols = slice(g * A_GROUP_DIM, (g + 1) * A_GROUP_DIM)
        for ch in range(h_ref.shape[0] // CHUNK):
            rows = slice(ch * CHUNK, (ch + 1) * CHUNK)
            mixed = jnp.dot(mixes[g], vnb[rows, cols], preferred_element_type=F32) + bias[:, cols]
            a_ref[rows, cols] = (gu[rows, cols] * mixed).astype(a_ref.dtype)


def _in_proj(h, w_in, a_ln_g, a_ln_b, mix, bias_full, rope, *, tm, table_blocks, sample_tokens):
    t = h.shape[0]
    sequences = (t // tm - 1) // table_blocks
    table = pl.BlockSpec((tm, 4 * LANES), lambda i: (jnp.where(i == 0, 0, 1 + (i - 1) % table_blocks), 0))
    tail = lambda width: pl.BlockSpec((1, WINDOW, width), lambda i: (jnp.maximum(i - 1, 0) // table_blocks, 0, 0))
    return pl.pallas_call(
        functools.partial(_in_proj_kernel, sample_tokens=sample_tokens),
        grid=(t // tm,),
        in_specs=[_row_spec(tm, D_MODEL), _resident(w_in.shape), _resident(a_ln_g.shape), _resident(a_ln_b.shape),
                  _resident(mix.shape), _resident(bias_full.shape), table],
        out_specs=[_row_spec(tm, A_WIDTH), _row_spec(tm, Q_WIDTH), _row_spec(tm, KV_WIDTH), _row_spec(tm, KV_WIDTH),
                   pl.BlockSpec((1, tm, A_WIDTH), lambda i: (jnp.minimum(i, 1), 0, 0)),
                   tail(A_WIDTH), tail(KV_WIDTH), tail(KV_WIDTH)],
        out_shape=[jax.ShapeDtypeStruct((t, A_WIDTH), BF16), jax.ShapeDtypeStruct((t, Q_WIDTH), BF16),
                   jax.ShapeDtypeStruct((t, KV_WIDTH), F32), jax.ShapeDtypeStruct((t, KV_WIDTH), F32),
                   jax.ShapeDtypeStruct((2, tm, A_WIDTH), F32), jax.ShapeDtypeStruct((sequences, WINDOW, A_WIDTH), F32),
                   jax.ShapeDtypeStruct((sequences, WINDOW, KV_WIDTH), F32),
                   jax.ShapeDtypeStruct((sequences, WINDOW, KV_WIDTH), F32)],
        compiler_params=pltpu.CompilerParams(
            dimension_semantics=("arbitrary",), vmem_limit_bytes=VMEM_LIMIT_BYTES),
        name="in_proj",
    )(h, w_in, a_ln_g, a_ln_b, mix, bias_full, rope)


def _kv_tiles(k2, v2, kv_head):
    low = lax.broadcasted_iota(jnp.int32, k2.shape, 1) < HEAD_DIM
    k2r = pltpu.roll(k2, HEAD_DIM, axis=1)
    v2r = pltpu.roll(v2, HEAD_DIM, axis=1)
    if kv_head == 0:
        keys, vals = jnp.where(low, k2, k2r), jnp.where(low, v2, v2r)
    else:
        keys, vals = jnp.where(low, k2r, k2), jnp.where(low, v2r, v2)
    vals = jnp.concatenate([vals, jnp.ones_like(vals)], axis=1)
    return keys.astype(BF16), vals.astype(BF16)


def _kv_attention(q_pieces, sink_pieces, keys, vals, bias):
    n, rows = len(q_pieces), q_pieces[0].shape[0]
    q = jnp.concatenate(q_pieces, axis=0) if n > 1 else q_pieces[0]
    q_low = lax.broadcasted_iota(jnp.int32, q.shape, 1) < HEAD_DIM
    zero = jnp.zeros_like(q)
    q_all = jnp.concatenate([jnp.where(q_low, q, zero), jnp.where(q_low, zero, q)], axis=0)
    s_all = lax.dot_general(q_all, keys, (((1,), (1,)), ((), ())), preferred_element_type=F32)
    probs, sink_terms = [], []
    for parity in range(2):
        for piece in range(n):
            lo = (parity * n + piece) * rows
            s = s_all[lo:lo + rows] + bias
            m = jnp.max(s, axis=-1, keepdims=True)
            probs.append(jnp.exp2(s - m).astype(BF16))
            sink_terms.append(jnp.exp2(sink_pieces[parity][piece] - m))
    o_all = jnp.dot(jnp.concatenate(probs, axis=0), vals, preferred_element_type=F32)
    o_low = lax.broadcasted_iota(jnp.int32, (rows, LANES), 1) < HEAD_DIM
    outs = []
    for piece in range(n):
        even = o_all[piece * rows:(piece + 1) * rows]
        odd = o_all[(n + piece) * rows:(n + piece + 1) * rows]
        num = jnp.where(o_low, even[:, :LANES], odd[:, :LANES])
        den = jnp.where(o_low, even[:, LANES:] + sink_terms[piece], odd[:, LANES:] + sink_terms[n + piece])
        outs.append(num / den)
    return outs


def _attn_prompt_kernel(sink_ref, bias_ref, q_ref, kp_ref, ko_ref, vp_ref, vo_ref, o_ref):
    slabs = GROUP // 2
    for blk in range(q_ref.shape[0] // WINDOW):
        rows = slice(blk * WINDOW, (blk + 1) * WINDOW)
        if blk == 0:
            bias = bias_ref[jnp.where(pl.program_id(1) > 0, 0, 1)]
            k_prev, v_prev = kp_ref[...], vp_ref[...]
        else:
            bias = bias_ref[0]
            k_prev, v_prev = ko_ref[(blk - 1) * WINDOW:blk * WINDOW, :], vo_ref[(blk - 1) * WINDOW:blk * WINDOW, :]
        k2 = jnp.concatenate([k_prev, ko_ref[rows, :]], axis=0)
        v2 = jnp.concatenate([v_prev, vo_ref[rows, :]], axis=0)
        for kh in range(N_KV_HEADS):
            keys, vals = _kv_tiles(k2, v2, kh)
            cols = [slice((kh * slabs + s) * LANES, (kh * slabs + s + 1) * LANES) for s in range(slabs)]
            sinks = [[sink_ref[kh * GROUP + 2 * s + parity] * LOG2E for s in range(slabs)] for parity in range(2)]
            outs = _kv_attention([q_ref[rows, c] for c in cols], sinks, keys, vals, bias)
            for c, o in zip(cols, outs):
                o_ref[rows, c] = o.astype(o_ref.dtype)


def _band_bias(query_pos, first_block):
    i = np.asarray(query_pos)[:, None]
    j = np.arange(2 * WINDOW)[None, :]
    prev_ok = (j < WINDOW) & (j > i) & (not first_block)
    own_ok = (j >= WINDOW) & (j - WINDOW <= i)
    return np.where(prev_ok | own_ok, 0.0, NEG_INF).astype(np.float32)


def _attn_prompt(sinks, q, k, v, *, batch, seq, tq, skip_rows):
    steps = seq // tq
    per_step = tq // WINDOW
    skip = skip_rows // tq
    bias = jnp.asarray(np.stack([_band_bias(np.arange(WINDOW), False), _band_bias(np.arange(WINDOW), True)]))
    own = lambda b, i: (skip + b * steps + i, 0)
    prev = lambda b, i: ((skip + b * steps + i) * per_step - jnp.minimum(i, 1), 0)
    return pl.pallas_call(
        _attn_prompt_kernel,
        grid=(batch, steps),
        in_specs=[pl.BlockSpec(memory_space=pltpu.SMEM), _resident(bias.shape),
                  pl.BlockSpec((tq, Q_WIDTH), own),
                  pl.BlockSpec((WINDOW, KV_WIDTH), prev), pl.BlockSpec((tq, KV_WIDTH), own),
                  pl.BlockSpec((WINDOW, KV_WIDTH), prev), pl.BlockSpec((tq, KV_WIDTH), own)],
        out_specs=pl.BlockSpec((tq, Q_WIDTH), lambda b, i: (b * steps + i, 0)),
        out_shape=jax.ShapeDtypeStruct((batch * seq, Q_WIDTH), BF16),
        compiler_params=pltpu.CompilerParams(
            dimension_semantics=("arbitrary", "arbitrary"), vmem_limit_bytes=VMEM_LIMIT_BYTES),
        name="attn_prompt",
    )(sinks, bias, q, k, k, v, v)


def _attn_sample_kernel(sink_ref, bias_ref, q_ref, kn_ref, vn_ref, kc_ref, vc_ref, o_ref, wk_ref, wv_ref, *,
                        dec_seq):
    rows = (GROUP // 2) * dec_seq
    key_pad = jnp.zeros((WINDOW - SAMPLE_KEY_PAD, KV_WIDTH), F32)
    bias = bias_ref[...]
    slab = lax.broadcasted_iota(jnp.int32, (rows, 1), 0) // dec_seq
    sinks = []
    for kh in range(N_KV_HEADS):
        per_parity = []
        for parity in range(2):
            sink = jnp.zeros((rows, 1), F32)
            for s in range(GROUP // 2):
                sink = jnp.where(slab == s, sink_ref[kh * GROUP + 2 * s + parity] * LOG2E, sink)
            per_parity.append([sink])
        sinks.append(per_parity)
    for b in range(q_ref.shape[0]):
        kc, vc, kn, vn = kc_ref[b], vc_ref[b], kn_ref[b], vn_ref[b]
        wk_ref[b, 0:WINDOW - dec_seq, :] = kc[dec_seq:, :]
        wk_ref[b, WINDOW - dec_seq:WINDOW, :] = kn[0:dec_seq, :]
        wv_ref[b, 0:WINDOW - dec_seq, :] = vc[dec_seq:, :]
        wv_ref[b, WINDOW - dec_seq:WINDOW, :] = vn[0:dec_seq, :]
        k2 = jnp.concatenate([kc, kn, key_pad], axis=0)
        v2 = jnp.concatenate([vc, vn, key_pad], axis=0)
        for kh in range(N_KV_HEADS):
            keys, vals = _kv_tiles(k2, v2, kh)
            q_rows = q_ref[b, kh * rows:(kh + 1) * rows, :]
            (out,) = _kv_attention([q_rows], sinks[kh], keys, vals, bias)
            o_ref[b, kh * rows:(kh + 1) * rows, :] = out.astype(o_ref.dtype)


def _attn_sample(sinks, q, k_new, v_new, k_cache, v_cache, *, dec_seq):
    nb = q.shape[0]
    bt = SAMPLE_BATCH_TILE
    blk = lambda rows: pl.BlockSpec((bt, rows, LANES), lambda i: (i, 0, 0))
    q_rows = N_KV_HEADS * (GROUP // 2) * dec_seq
    bias = jnp.asarray(_band_bias(np.arange((GROUP // 2) * dec_seq) % dec_seq, False))
    return pl.pallas_call(
        functools.partial(_attn_sample_kernel, dec_seq=dec_seq),
        grid=(nb // bt,),
        in_specs=[pl.BlockSpec(memory_space=pltpu.SMEM), _resident(bias.shape), blk(q_rows),
                  blk(SAMPLE_KEY_PAD), blk(SAMPLE_KEY_PAD), blk(WINDOW), blk(WINDOW)],
        out_specs=[blk(q_rows), blk(WINDOW), blk(WINDOW)],
        out_shape=[jax.ShapeDtypeStruct((nb, q_rows, LANES), BF16),
                   jax.ShapeDtypeStruct((nb, WINDOW, KV_WIDTH), F32),
                   jax.ShapeDtypeStruct((nb, WINDOW, KV_WIDTH), F32)],
        compiler_params=pltpu.CompilerParams(
            dimension_semantics=("arbitrary",), vmem_limit_bytes=VMEM_LIMIT_BYTES),
        name="attn_sample",
    )(sinks, bias, q, k_new, v_new, k_cache, v_cache)


def _merge_kernel(h_ref, a_ref, bs_ref, bp_ref, win_ref, wpa_ref, wpb_ref, wo_ref, g_ref, bb_ref, o_ref):
    h = h_ref[...]
    hb = h.astype(BF16)
    a = a_ref[...]
    b = _load_rows((bs_ref, bp_ref))
    merged = []
    for lo in range(0, D_MODEL, PROJ_GROUP):
        cols = slice(lo, lo + PROJ_GROUP)
        gate_a_cols = slice(MIX_WIDTH + lo, MIX_WIDTH + lo + PROJ_GROUP)
        gate_b_cols = slice(MIX_WIDTH + D_MODEL + lo, MIX_WIDTH + D_MODEL + lo + PROJ_GROUP)
        gate_a = jax.nn.sigmoid(jnp.dot(hb, win_ref[:, gate_a_cols], preferred_element_type=F32))
        part = gate_a * jnp.dot(a, wpa_ref[:, cols], preferred_element_type=F32)
        gate_b = jax.nn.sigmoid(jnp.dot(hb, win_ref[:, gate_b_cols], preferred_element_type=F32))
        part = part + gate_b * jnp.dot(b, wpb_ref[:, cols], preferred_element_type=F32)
        merged.append(part.astype(BF16))
    y = jnp.dot(jnp.concatenate(merged, axis=1), wo_ref[...], preferred_element_type=F32)
    o_ref[...] = _layernorm(DN_ALPHA * h + y, g_ref[...], bb_ref[...])


def _merge(h, a_out, b_sample, b_prompt, w_in, w_pa, w_pb, w_o, g, b, *, tm):
    t = h.shape[0]
    return pl.pallas_call(
        _merge_kernel,
        grid=(t // tm,),
        in_specs=[_row_spec(tm, D_MODEL), _row_spec(tm, A_WIDTH), *_stream_specs(tm, Q_WIDTH), _resident(w_in.shape),
                  _resident(w_pa.shape), _resident(w_pb.shape), _resident(w_o.shape), _resident(g.shape),
                  _resident(b.shape)],
        out_specs=_row_spec(tm, D_MODEL),
        out_shape=jax.ShapeDtypeStruct((t, D_MODEL), F32),
        compiler_params=pltpu.CompilerParams(
            dimension_semantics=("arbitrary",), vmem_limit_bytes=VMEM_LIMIT_BYTES),
        name="merge",
    )(h, a_out, b_sample, b_prompt, w_in, w_pa, w_pb, w_o, g, b)


def _rope_tables(positions):
    half = HEAD_DIM // 2
    inv = ROPE_THETA ** (-np.arange(half, dtype=np.float64) / half)
    ang = np.asarray(positions, np.float64)[:, None] * inv[None, :]
    cos = np.tile(np.cos(ang), (1, 2 * LANES // HEAD_DIM))
    sin = np.tile(np.concatenate([-np.sin(ang), np.sin(ang)], axis=1), (1, LANES // HEAD_DIM))
    q_scale = LOG2E * HEAD_DIM ** -0.5
    return jnp.asarray(np.concatenate([cos, sin, cos * q_scale, sin * q_scale], axis=1), F32)


def kernel(x_prompt, x_sample, cache_win_k, cache_win_v, ffn1_up, ffn1_down, ln1_g, ln1_b, w_in, a_ln_g, a_ln_b,
           a_ws, a_bs, attn_sinks, w_pa, w_pb, w_o, ln2_g, ln2_b, ffn2_up, ffn2_down, ln3_g, ln3_b):
    batch, seq, _ = x_prompt.shape
    dec_batch, dec_seq, _ = x_sample.shape
    buf = cache_win_k.shape[2]
    assert DEPTH == 1 and buf == WINDOW and seq % ROW_TILE == 0 and ROW_TILE % CHUNK == 0 and seq % ATTN_Q_TILE == 0
    assert CHUNK % dec_seq == 0 and (dec_batch * dec_seq) % CHUNK == 0 and dec_batch % SAMPLE_BATCH_TILE == 0
    assert dec_seq & (dec_seq - 1) == 0
    tp, ts = batch * seq, dec_batch * dec_seq
    assert ts == ROW_TILE
    assert seq % CHUNK == 0 and CHUNK == WINDOW

    up1, dn1 = ffn1_up[0].astype(BF16), ffn1_down[0].astype(BF16)
    sinks = attn_sinks[0]
    mix = a_ws[0]
    mix_bias = jnp.repeat(a_bs[0].T, A_GROUP_DIM, axis=1)
    rope = _rope_tables(np.concatenate([PAST_LEN + np.arange(ts) % dec_seq, np.arange(seq)]))

    xs, xp = x_sample.reshape(ts, D_MODEL), x_prompt.reshape(tp, D_MODEL)
    (h,), (up2, dn2, w_in_b, wpa, wpb, wo) = _ffn_ln(
        (xs, xp), up1, dn1, ln1_g, ln1_b, tm=ROW_TILE, out_rows=ts + tp,
        cast=(ffn2_up[0], ffn2_down[0], w_in[0], w_pa[0], w_pb[0], w_o[0]))
    a_out, q, k, v, vn_first, vn_tail, k_tail, v_tail = _in_proj(
        h, w_in_b, a_ln_g, a_ln_b, mix, mix_bias, rope, tm=ROW_TILE, table_blocks=seq // ROW_TILE,
        sample_tokens=dec_seq)

    b_prompt = _attn_prompt(sinks, q, k, v, batch=batch, seq=seq, tq=ATTN_Q_TILE, skip_rows=ts)
    slabs = Q_WIDTH // LANES
    qs_t = q[:ts].reshape(dec_batch, dec_seq, slabs, LANES).transpose(0, 2, 1, 3).reshape(
        dec_batch, slabs * dec_seq, LANES)
    pad = ((0, 0), (0, SAMPLE_KEY_PAD - dec_seq), (0, 0))
    ks_pad = jnp.pad(k[:ts].reshape(dec_batch, dec_seq, KV_WIDTH), pad)
    vs_pad = jnp.pad(v[:ts].reshape(dec_batch, dec_seq, KV_WIDTH), pad)
    bs_t, win_k_s, win_v_s = _attn_sample(
        sinks, qs_t, ks_pad, vs_pad, cache_win_k[0].reshape(dec_batch, buf, KV_WIDTH),
        cache_win_v[0].reshape(dec_batch, buf, KV_WIDTH), dec_seq=dec_seq)
    b_sample = bs_t.reshape(dec_batch, slabs, dec_seq, LANES).transpose(0, 2, 1, 3).reshape(ts, Q_WIDTH)

    h2 = _merge(h, a_out, b_sample, b_prompt, w_in_b, wpa, wpb, wo, ln2_g, ln2_b, tm=ROW_TILE)
    (ys, yp), _ = _ffn_ln(h2, up2, dn2, ln3_g, ln3_b, tm=ROW_TILE, out_rows=(ts, tp))

    kv_shape = (DEPTH, -1, buf, N_KV_HEADS, HEAD_DIM)
    return (yp.reshape(batch, seq, D_MODEL), ys.reshape(dec_batch, dec_seq, D_MODEL),
            k_tail.reshape(kv_shape), v_tail.reshape(kv_shape),
            win_k_s.reshape(kv_shape), win_v_s.reshape(kv_shape),
            vn_tail[None], vn_first[0].reshape(DEPTH, dec_batch, dec_seq, A_WIDTH))
```

```python
import functools

import numpy as np
import jax
import jax.numpy as jnp
from jax import lax
from jax.experimental import pallas as pl
from jax.experimental.pallas import tpu as pltpu

F32 = jnp.float32
BF16 = jnp.bfloat16

D_MODEL = 1024
DEPTH = 1
PAST_LEN = 16384
N_HEADS = 16
N_KV_HEADS = 2
HEAD_DIM = 64
GROUP = N_HEADS // N_KV_HEADS
WINDOW = 128
ROPE_THETA = 10000.0
CHUNK = 128
A_GROUPS = 4
A_GROUP_DIM = 128
A_WIDTH = A_GROUPS * A_GROUP_DIM
D_FF = 2816
LN_EPS = 1e-5
NEG_INF = -1e30
DN_ALPHA = (2.0 * DEPTH) ** 0.25
LOG2E = 1.4426950408889634
Q_WIDTH = N_HEADS * HEAD_DIM
KV_WIDTH = N_KV_HEADS * HEAD_DIM
MIX_WIDTH = 2 * A_WIDTH + Q_WIDTH + 2 * KV_WIDTH
IN_WIDTH = MIX_WIDTH + 2 * D_MODEL

LANES = 128
BF16_TILE_ROWS = 16
VMEM_LIMIT_BYTES = 56 * 1024 * 1024

ROW_TILE = 512
FF_CHUNK = 256
PROJ_GROUP = 256
SAMPLE_KEY_PAD = 16
SAMPLE_BATCH_TILE = 16


def _resident(shape):
    nd = len(shape)
    return pl.BlockSpec(shape, lambda *_: (0,) * nd, pipeline_mode=pl.Buffered(1))


def _row_spec(tm, width):
    return pl.BlockSpec((tm, width), lambda i: (i, 0))


def _stream_specs(tm, width):
    return (pl.BlockSpec((tm, width), lambda i: (0, 0)),
            pl.BlockSpec((tm, width), lambda i: (jnp.maximum(i - 1, 0), 0)))


def _load_rows(refs):
    if len(refs) == 1:
        return refs[0][...]
    return jnp.where(pl.program_id(0) == 0, refs[0][...], refs[1][...])


def _store_rows(refs, value):
    if len(refs) == 1:
        refs[0][...] = value.astype(refs[0].dtype)
        return
    sample_ref, prompt_ref = refs
    prompt_ref[...] = value.astype(prompt_ref.dtype)

    @pl.when(pl.program_id(0) == 0)
    def _():
        sample_ref[...] = prompt_ref[...]


def _layernorm(y, g, b):
    mu = jnp.mean(y, axis=-1, keepdims=True)
    d = y - mu
    var = jnp.mean(d * d, axis=-1, keepdims=True)
    return d * lax.rsqrt(var + LN_EPS) * g + b


def _ffn_ln_kernel(*refs, n_in, n_out):
    x_refs, (wup_ref, wdn_ref, g_ref, b_ref), rest = refs[:n_in], refs[n_in:n_in + 4], refs[n_in + 4:]
    n_cast = (len(rest) - n_out) // 2
    cast_in, o_refs, cast_out = rest[:n_cast], rest[n_cast:n_cast + n_out], rest[n_cast + n_out:]
    x = _load_rows(x_refs).astype(F32)
    xb = x.astype(BF16)
    y = None
    for c in range(D_FF // FF_CHUNK):
        lo = c * FF_CHUNK
        gate = jnp.dot(xb, wup_ref[:, lo:lo + FF_CHUNK], preferred_element_type=F32)
        up = jnp.dot(xb, wup_ref[:, D_FF + lo:D_FF + lo + FF_CHUNK], preferred_element_type=F32)
        act = (gate * jax.nn.sigmoid(gate) * up).astype(BF16)
        part = jnp.dot(act, wdn_ref[lo:lo + FF_CHUNK, :], preferred_element_type=F32)
        y = part if y is None else y + part
    r = DN_ALPHA * x + 0.5 * y
    _store_rows(o_refs, _layernorm(r, g_ref[...], b_ref[...]))
    for src, dst in zip(cast_in, cast_out):
        dst[...] = src[...].astype(dst.dtype)


def _cast_chunk_rows(rows, steps):
    return next(c for c in range(BF16_TILE_ROWS, rows + 1, BF16_TILE_ROWS)
                if rows % c == 0 and rows // c <= steps)


def _ffn_ln(xs, w_up, w_down, g, b, *, tm, out_rows, cast=()):
    xs = xs if isinstance(xs, tuple) else (xs,)
    outs = out_rows if isinstance(out_rows, tuple) else (out_rows,)
    steps = sum(outs) // tm
    pick = lambda n: _stream_specs(tm, D_MODEL) if n == 2 else (_row_spec(tm, D_MODEL),)
    cast_specs = []
    for w in cast:
        chunk = _cast_chunk_rows(w.shape[0], steps)
        last = w.shape[0] // chunk - 1
        cast_specs.append(pl.BlockSpec((chunk, w.shape[1]), lambda i, last=last: (jnp.minimum(i, last), 0)))
    out = pl.pallas_call(
        functools.partial(_ffn_ln_kernel, n_in=len(xs), n_out=len(outs)),
        grid=(steps,),
        in_specs=[*pick(len(xs)), _resident(w_up.shape), _resident(w_down.shape),
                  _resident(g.shape), _resident(b.shape)] + cast_specs,
        out_specs=[*pick(len(outs))] + cast_specs,
        out_shape=[jax.ShapeDtypeStruct((r, D_MODEL), F32) for r in outs]
        + [jax.ShapeDtypeStruct(w.shape, BF16) for w in cast],
        compiler_params=pltpu.CompilerParams(
            dimension_semantics=("arbitrary",), vmem_limit_bytes=VMEM_LIMIT_BYTES),
        name="ffn_ln",
    )(*xs, w_up, w_down, g, b, *cast)
    return tuple(out[:len(outs)]), tuple(out[len(outs):])


def _rope(x, cos, sin_signed, first_half):
    partner = jnp.where(first_half, pltpu.roll(x, LANES - HEAD_DIM // 2, axis=1),
                        pltpu.roll(x, HEAD_DIM // 2, axis=1))
    return x * cos + partner * sin_signed


def _spatial_mix(mix_ref, bias_ref, chunk_tokens):
    r = lax.broadcasted_iota(jnp.int32, (CHUNK, CHUNK), 0)
    c = lax.broadcasted_iota(jnp.int32, (CHUNK, CHUNK), 1)
    if chunk_tokens == CHUNK:
        return tuple(jnp.where(c <= r, mix_ref[g], 0.0).astype(BF16) for g in range(A_GROUPS)), bias_ref[...]
    token_mask = chunk_tokens - 1
    pick = jnp.where(((r & token_mask) == c) & (c < chunk_tokens), 1.0, 0.0).astype(BF16)
    keep = (c <= r) & ((c | token_mask) == (r | token_mask))
    mixes = []
    for g in range(A_GROUPS):
        rows_of_token = jnp.dot(pick, mix_ref[g].astype(BF16), preferred_element_type=F32).astype(BF16)
        tiled = lax.dot_general(rows_of_token, pick, (((1,), (1,)), ((), ())), preferred_element_type=F32)
        mixes.append(jnp.where(keep, tiled, 0.0).astype(BF16))
    token = lax.broadcasted_iota(jnp.int32, (CHUNK, A_WIDTH), 0) & token_mask
    bias = jnp.zeros((CHUNK, A_WIDTH), F32)
    for t in range(chunk_tokens):
        bias = jnp.where(token == t, bias_ref[t:t + 1, :], bias)
    return tuple(mixes), bias


def _in_proj_kernel(h_ref, w_ref, lng_ref, lnb_ref, mix_ref, bias_ref, rope_ref,
                    a_ref, q_ref, k_ref, v_ref, vn_first_ref, vn_tail_ref, k_tail_ref, v_tail_ref, *, sample_tokens):
    mixes, bias = lax.cond(pl.program_id(0) == 0,
                           lambda: _spatial_mix(mix_ref, bias_ref, sample_tokens),
                           lambda: _spatial_mix(mix_ref, bias_ref, CHUNK))
    hb = h_ref[...].astype(BF16)
    cos_k, sin_k, cos_q, sin_q = [rope_ref[:, i * LANES:(i + 1) * LANES] for i in range(4)]
    lane = lax.broadcasted_iota(jnp.int32, cos_k.shape, 1)
    first_half = (lane & (HEAD_DIM - 1)) < (HEAD_DIM // 2)

    def project(lo):
        return jnp.dot(hb, w_ref[:, lo:lo + PROJ_GROUP], preferred_element_type=F32)

    def rotary_q(i):
        x = project(2 * A_WIDTH + i * PROJ_GROUP)
        for j in range(PROJ_GROUP // LANES):
            cols = slice(i * PROJ_GROUP + j * LANES, i * PROJ_GROUP + (j + 1) * LANES)
            q_ref[:, cols] = _rope(x[:, j * LANES:(j + 1) * LANES], cos_q, sin_q, first_half).astype(q_ref.dtype)

    gu, gv = [], []
    for i in range(A_WIDTH // PROJ_GROUP):
        gu.append(jax.nn.gelu(project(i * PROJ_GROUP)))
        rotary_q(2 * i)
        gv.append(jax.nn.gelu(project(A_WIDTH + i * PROJ_GROUP)))
        rotary_q(2 * i + 1)
    kv = project(2 * A_WIDTH + Q_WIDTH)
    tail = slice(h_ref.shape[0] - WINDOW, h_ref.shape[0])
    k_ref[...] = _rope(kv[:, :KV_WIDTH], cos_k, sin_k, first_half)
    v_ref[...] = kv[:, KV_WIDTH:]
    k_tail_ref[0] = k_ref[tail, :]
    v_tail_ref[0] = v_ref[tail, :]

    gu = jnp.concatenate(gu, axis=1)
    vn = _layernorm(jnp.concatenate(gv, axis=1), lng_ref[...], lnb_ref[...])
    vn_tail_ref[0] = vn[tail, :]
    vn_first_ref[0] = vn
    vnb = vn.astype(BF16)
    for g in range(A_GROUPS):
        cols = slice(g * A_GROUP_DIM, (g + 1) * A_GROUP_DIM)
        for ch in range(h_ref.shape[0] // CHUNK):
            rows = slice(ch * CHUNK, (ch + 1) * CHUNK)
            mixed = jnp.dot(mixes[g], vnb[rows, cols], preferred_element_type=F32) + bias[:, cols]
            a_ref[rows, cols] = (gu[rows, cols] * mixed).astype(a_ref.dtype)


def _in_proj(h, w_in, a_ln_g, a_ln_b, mix, bias_full, rope, *, tm, table_blocks, sample_tokens):
    t = h.shape[0]
    sequences = (t // tm - 1) // table_blocks
    table = pl.BlockSpec((tm, 4 * LANES), lambda i: (jnp.where(i == 0, 0, 1 + (i - 1) % table_blocks), 0))
    tail = lambda width: pl.BlockSpec((1, WINDOW, width), lambda i: (jnp.maximum(i - 1, 0) // table_blocks, 0, 0))
    return pl.pallas_call(
        functools.partial(_in_proj_kernel, sample_tokens=sample_tokens),
        grid=(t // tm,),
        in_specs=[_row_spec(tm, D_MODEL), _resident(w_in.shape), _resident(a_ln_g.shape), _resident(a_ln_b.shape),
                  _resident(mix.shape), _resident(bias_full.shape), table],
        out_specs=[_row_spec(tm, A_WIDTH), _row_spec(tm, Q_WIDTH), _row_spec(tm, KV_WIDTH), _row_spec(tm, KV_WIDTH),
                   pl.BlockSpec((1, tm, A_WIDTH), lambda i: (jnp.minimum(i, 1), 0, 0)),
                   tail(A_WIDTH), tail(KV_WIDTH), tail(KV_WIDTH)],
        out_shape=[jax.ShapeDtypeStruct((t, A_WIDTH), BF16), jax.ShapeDtypeStruct((t, Q_WIDTH), BF16),
                   jax.ShapeDtypeStruct((t, KV_WIDTH), F32), jax.ShapeDtypeStruct((t, KV_WIDTH), F32),
                   jax.ShapeDtypeStruct((2, tm, A_WIDTH), F32), jax.ShapeDtypeStruct((sequences, WINDOW, A_WIDTH), F32),
                   jax.ShapeDtypeStruct((sequences, WINDOW, KV_WIDTH), F32),
                   jax.ShapeDtypeStruct((sequences, WINDOW, KV_WIDTH), F32)],
        compiler_params=pltpu.CompilerParams(
            dimension_semantics=("arbitrary",), vmem_limit_bytes=VMEM_LIMIT_BYTES),
        name="in_proj",
    )(h, w_in, a_ln_g, a_ln_b, mix, bias_full, rope)


def _kv_tiles(k2, v2, kv_head):
    low = lax.broadcasted_iota(jnp.int32, k2.shape, 1) < HEAD_DIM
    k2r = pltpu.roll(k2, HEAD_DIM, axis=1)
    v2r = pltpu.roll(v2, HEAD_DIM, axis=1)
    if kv_head == 0:
        keys, vals = jnp.where(low, k2, k2r), jnp.where(low, v2, v2r)
    else:
        keys, vals = jnp.where(low, k2r, k2), jnp.where(low, v2r, v2)
    vals = jnp.concatenate([vals, jnp.ones_like(vals)], axis=1)
    return keys.astype(BF16), vals.astype(BF16)


def _kv_attention(q_pieces, sink_pieces, keys, vals, bias):
    n, rows = len(q_pieces), q_pieces[0].shape[0]
    q = jnp.concatenate(q_pieces, axis=0) if n > 1 else q_pieces[0]
    q_low = lax.broadcasted_iota(jnp.int32, q.shape, 1) < HEAD_DIM
    zero = jnp.zeros_like(q)
    q_all = jnp.concatenate([jnp.where(q_low, q, zero), jnp.where(q_low, zero, q)], axis=0)
    s_all = lax.dot_general(q_all, keys, (((1,), (1,)), ((), ())), preferred_element_type=F32)
    probs, sink_terms = [], []
    for parity in range(2):
        for piece in range(n):
            lo = (parity * n + piece) * rows
            s = s_all[lo:lo + rows] + bias
            m = jnp.max(s, axis=-1, keepdims=True)
            probs.append(jnp.exp2(s - m).astype(BF16))
            sink_terms.append(jnp.exp2(sink_pieces[parity][piece] - m))
    o_all = jnp.dot(jnp.concatenate(probs, axis=0), vals, preferred_element_type=F32)
    o_low = lax.broadcasted_iota(jnp.int32, (rows, LANES), 1) < HEAD_DIM
    outs = []
    for piece in range(n):
        even = o_all[piece * rows:(piece + 1) * rows]
        odd = o_all[(n + piece) * rows:(n + piece + 1) * rows]
        num = jnp.where(o_low, even[:, :LANES], odd[:, :LANES])
        den = jnp.where(o_low, even[:, LANES:] + sink_terms[piece], odd[:, LANES:] + sink_terms[n + piece])
        outs.append(num / den)
    return outs


def _attend_block(blk, sink_ref, bias, q_ref, k_prev, v_prev, k_ref, v_ref, o_ref):
    slabs = GROUP // 2
    rows = slice(blk * WINDOW, (blk + 1) * WINDOW)
    k2 = jnp.concatenate([k_prev, k_ref[rows, :]], axis=0)
    v2 = jnp.concatenate([v_prev, v_ref[rows, :]], axis=0)
    for kh in range(N_KV_HEADS):
        keys, vals = _kv_tiles(k2, v2, kh)
        cols = [slice((kh * slabs + s) * LANES, (kh * slabs + s + 1) * LANES) for s in range(slabs)]
        sinks = [[sink_ref[kh * GROUP + 2 * s + parity] * LOG2E for s in range(slabs)] for parity in range(2)]
        outs = _kv_attention([q_ref[rows, c] for c in cols], sinks, keys, vals, bias)
        for c, o in zip(cols, outs):
            o_ref[rows, c] = o.astype(o_ref.dtype)


def _band_bias(query_pos, first_block):
    i = np.asarray(query_pos)[:, None]
    j = np.arange(2 * WINDOW)[None, :]
    prev_ok = (j < WINDOW) & (j > i) & (not first_block)
    own_ok = (j >= WINDOW) & (j - WINDOW <= i)
    return np.where(prev_ok | own_ok, 0.0, NEG_INF).astype(np.float32)


def _attn_sample_kernel(sink_ref, bias_ref, q_ref, kn_ref, vn_ref, kc_ref, vc_ref, o_ref, wk_ref, wv_ref, *,
                        dec_seq):
    rows = (GROUP // 2) * dec_seq
    key_pad = jnp.zeros((WINDOW - SAMPLE_KEY_PAD, KV_WIDTH), F32)
    bias = bias_ref[...]
    slab = lax.broadcasted_iota(jnp.int32, (rows, 1), 0) // dec_seq
    sinks = []
    for kh in range(N_KV_HEADS):
        per_parity = []
        for parity in range(2):
            sink = jnp.zeros((rows, 1), F32)
            for s in range(GROUP // 2):
                sink = jnp.where(slab == s, sink_ref[kh * GROUP + 2 * s + parity] * LOG2E, sink)
            per_parity.append([sink])
        sinks.append(per_parity)
    for b in range(q_ref.shape[0]):
        kc, vc, kn, vn = kc_ref[b], vc_ref[b], kn_ref[b], vn_ref[b]
        wk_ref[b, 0:WINDOW - dec_seq, :] = kc[dec_seq:, :]
        wk_ref[b, WINDOW - dec_seq:WINDOW, :] = kn[0:dec_seq, :]
        wv_ref[b, 0:WINDOW - dec_seq, :] = vc[dec_seq:, :]
        wv_ref[b, WINDOW - dec_seq:WINDOW, :] = vn[0:dec_seq, :]
        k2 = jnp.concatenate([kc, kn, key_pad], axis=0)
        v2 = jnp.concatenate([vc, vn, key_pad], axis=0)
        for kh in range(N_KV_HEADS):
            keys, vals = _kv_tiles(k2, v2, kh)
            q_rows = q_ref[b, kh * rows:(kh + 1) * rows, :]
            (out,) = _kv_attention([q_rows], sinks[kh], keys, vals, bias)
            o_ref[b, kh * rows:(kh + 1) * rows, :] = out.astype(o_ref.dtype)


def _attn_sample(sinks, q, k_new, v_new, k_cache, v_cache, *, dec_seq):
    nb = q.shape[0]
    bt = SAMPLE_BATCH_TILE
    blk = lambda rows: pl.BlockSpec((bt, rows, LANES), lambda i: (i, 0, 0))
    q_rows = N_KV_HEADS * (GROUP // 2) * dec_seq
    bias = jnp.asarray(_band_bias(np.arange((GROUP // 2) * dec_seq) % dec_seq, False))
    return pl.pallas_call(
        functools.partial(_attn_sample_kernel, dec_seq=dec_seq),
        grid=(nb // bt,),
        in_specs=[pl.BlockSpec(memory_space=pltpu.SMEM), _resident(bias.shape), blk(q_rows),
                  blk(SAMPLE_KEY_PAD), blk(SAMPLE_KEY_PAD), blk(WINDOW), blk(WINDOW)],
        out_specs=[blk(q_rows), blk(WINDOW), blk(WINDOW)],
        out_shape=[jax.ShapeDtypeStruct((nb, q_rows, LANES), BF16),
                   jax.ShapeDtypeStruct((nb, WINDOW, KV_WIDTH), F32),
                   jax.ShapeDtypeStruct((nb, WINDOW, KV_WIDTH), F32)],
        compiler_params=pltpu.CompilerParams(
            dimension_semantics=("arbitrary",), vmem_limit_bytes=VMEM_LIMIT_BYTES),
        name="attn_sample",
    )(sinks, bias, q, k_new, v_new, k_cache, v_cache)


def _mix_merge_kernel(sink_ref, bias_ref, h_ref, a_ref, q_ref, kp_ref, k_ref, vp_ref, v_ref, bs_ref,
                      win_ref, wpa_ref, wpb_ref, wo_ref, g_ref, bb_ref, o_ref, b_buf, *, tiles_per_seq):
    step = pl.program_id(0)
    first_of_sequence = (step - 1) % tiles_per_seq == 0
    h = h_ref[...]
    hb = h.astype(BF16)
    a = a_ref[...]
    gated_a, gates_b = [], []
    for blk in range(h_ref.shape[0] // WINDOW):
        if blk == 0:
            bias = bias_ref[jnp.where(first_of_sequence, 1, 0)]
            k_prev, v_prev = kp_ref[...], vp_ref[...]
        else:
            bias = bias_ref[0]
            k_prev, v_prev = k_ref[(blk - 1) * WINDOW:blk * WINDOW, :], v_ref[(blk - 1) * WINDOW:blk * WINDOW, :]
        _attend_block(blk, sink_ref, bias, q_ref, k_prev, v_prev, k_ref, v_ref, b_buf)
        cols = slice(blk * PROJ_GROUP, (blk + 1) * PROJ_GROUP)
        gate_a_cols = slice(MIX_WIDTH + cols.start, MIX_WIDTH + cols.stop)
        gate_b_cols = slice(MIX_WIDTH + D_MODEL + cols.start, MIX_WIDTH + D_MODEL + cols.stop)
        gate_a = jax.nn.sigmoid(jnp.dot(hb, win_ref[:, gate_a_cols], preferred_element_type=F32))
        gated_a.append(gate_a * jnp.dot(a, wpa_ref[:, cols], preferred_element_type=F32))
        gates_b.append(jax.nn.sigmoid(jnp.dot(hb, win_ref[:, gate_b_cols], preferred_element_type=F32)))
    b = jnp.where(step == 0, bs_ref[...], b_buf[...])
    merged = []
    for blk, (part, gate_b) in enumerate(zip(gated_a, gates_b)):
        cols = slice(blk * PROJ_GROUP, (blk + 1) * PROJ_GROUP)
        part = part + gate_b * jnp.dot(b, wpb_ref[:, cols], preferred_element_type=F32)
        merged.append(part.astype(BF16))
    y = jnp.dot(jnp.concatenate(merged, axis=1), wo_ref[...], preferred_element_type=F32)
    o_ref[...] = _layernorm(DN_ALPHA * h + y, g_ref[...], bb_ref[...])


def _mix_merge(sinks, h, a_out, q, k, v, b_sample, w_in, w_pa, w_pb, w_o, g, b, *, tm, tiles_per_seq):
    t = h.shape[0]
    assert tm // WINDOW == D_MODEL // PROJ_GROUP
    bias = jnp.asarray(np.stack([_band_bias(np.arange(WINDOW), False), _band_bias(np.arange(WINDOW), True)]))
    prev = pl.BlockSpec((WINDOW, KV_WIDTH), lambda i: (jnp.maximum(i * (tm // WINDOW) - 1, 0), 0))
    return pl.pallas_call(
        functools.partial(_mix_merge_kernel, tiles_per_seq=tiles_per_seq),
        grid=(t // tm,),
        in_specs=[pl.BlockSpec(memory_space=pltpu.SMEM), _resident(bias.shape),
                  _row_spec(tm, D_MODEL), _row_spec(tm, A_WIDTH), _row_spec(tm, Q_WIDTH),
                  prev, _row_spec(tm, KV_WIDTH), prev, _row_spec(tm, KV_WIDTH),
                  pl.BlockSpec((tm, Q_WIDTH), lambda i: (0, 0)),
                  _resident(w_in.shape), _resident(w_pa.shape), _resident(w_pb.shape), _resident(w_o.shape),
                  _resident(g.shape), _resident(b.shape)],
        out_specs=_row_spec(tm, D_MODEL),
        out_shape=jax.ShapeDtypeStruct((t, D_MODEL), F32),
        scratch_shapes=[pltpu.VMEM((tm, Q_WIDTH), BF16)],
        compiler_params=pltpu.CompilerParams(
            dimension_semantics=("arbitrary",), vmem_limit_bytes=VMEM_LIMIT_BYTES),
        name="mix_merge",
    )(sinks, bias, h, a_out, q, k, k, v, v, b_sample, w_in, w_pa, w_pb, w_o, g, b)


def _rope_tables(positions):
    half = HEAD_DIM // 2
    inv = ROPE_THETA ** (-np.arange(half, dtype=np.float64) / half)
    ang = np.asarray(positions, np.float64)[:, None] * inv[None, :]
    cos = np.tile(np.cos(ang), (1, 2 * LANES // HEAD_DIM))
    sin = np.tile(np.concatenate([-np.sin(ang), np.sin(ang)], axis=1), (1, LANES // HEAD_DIM))
    q_scale = LOG2E * HEAD_DIM ** -0.5
    return jnp.asarray(np.concatenate([cos, sin, cos * q_scale, sin * q_scale], axis=1), F32)


def kernel(x_prompt, x_sample, cache_win_k, cache_win_v, ffn1_up, ffn1_down, ln1_g, ln1_b, w_in, a_ln_g, a_ln_b,
           a_ws, a_bs, attn_sinks, w_pa, w_pb, w_o, ln2_g, ln2_b, ffn2_up, ffn2_down, ln3_g, ln3_b):
    batch, seq, _ = x_prompt.shape
    dec_batch, dec_seq, _ = x_sample.shape
    buf = cache_win_k.shape[2]
    assert DEPTH == 1 and buf == WINDOW and seq % ROW_TILE == 0 and ROW_TILE % CHUNK == 0
    assert CHUNK % dec_seq == 0 and (dec_batch * dec_seq) % CHUNK == 0 and dec_batch % SAMPLE_BATCH_TILE == 0
    assert dec_seq & (dec_seq - 1) == 0
    tp, ts = batch * seq, dec_batch * dec_seq
    assert ts == ROW_TILE
    assert seq % CHUNK == 0 and CHUNK == WINDOW

    up1, dn1 = ffn1_up[0].astype(BF16), ffn1_down[0].astype(BF16)
    sinks = attn_sinks[0]
    mix = a_ws[0]
    mix_bias = jnp.repeat(a_bs[0].T, A_GROUP_DIM, axis=1)
    rope = _rope_tables(np.concatenate([PAST_LEN + np.arange(ts) % dec_seq, np.arange(seq)]))

    xs, xp = x_sample.reshape(ts, D_MODEL), x_prompt.reshape(tp, D_MODEL)
    (h,), (up2, dn2, w_in_b, wpa, wpb, wo) = _ffn_ln(
        (xs, xp), up1, dn1, ln1_g, ln1_b, tm=ROW_TILE, out_rows=ts + tp,
        cast=(ffn2_up[0], ffn2_down[0], w_in[0], w_pa[0], w_pb[0], w_o[0]))
    a_out, q, k, v, vn_first, vn_tail, k_tail, v_tail = _in_proj(
        h, w_in_b, a_ln_g, a_ln_b, mix, mix_bias, rope, tm=ROW_TILE, table_blocks=seq // ROW_TILE,
        sample_tokens=dec_seq)

    slabs = Q_WIDTH // LANES
    qs_t = q[:ts].reshape(dec_batch, dec_seq, slabs, LANES).transpose(0, 2, 1, 3).reshape(
        dec_batch, slabs * dec_seq, LANES)
    pad = ((0, 0), (0, SAMPLE_KEY_PAD - dec_seq), (0, 0))
    ks_pad = jnp.pad(k[:ts].reshape(dec_batch, dec_seq, KV_WIDTH), pad)
    vs_pad = jnp.pad(v[:ts].reshape(dec_batch, dec_seq, KV_WIDTH), pad)
    bs_t, win_k_s, win_v_s = _attn_sample(
        sinks, qs_t, ks_pad, vs_pad, cache_win_k[0].reshape(dec_batch, buf, KV_WIDTH),
        cache_win_v[0].reshape(dec_batch, buf, KV_WIDTH), dec_seq=dec_seq)
    b_sample = bs_t.reshape(dec_batch, slabs, dec_seq, LANES).transpose(0, 2, 1, 3).reshape(ts, Q_WIDTH)

    h2 = _mix_merge(sinks, h, a_out, q, k, v, b_sample, w_in_b, wpa, wpb, wo, ln2_g, ln2_b, tm=ROW_TILE,
                    tiles_per_seq=seq // ROW_TILE)
    (ys, yp), _ = _ffn_ln(h2, up2, dn2, ln3_g, ln3_b, tm=ROW_TILE, out_rows=(ts, tp))

    kv_shape = (DEPTH, -1, buf, N_KV_HEADS, HEAD_DIM)
    return (yp.reshape(batch, seq, D_MODEL), ys.reshape(dec_batch, dec_seq, D_MODEL),
            k_tail.reshape(kv_shape), v_tail.reshape(kv_shape),
            win_k_s.reshape(kv_shape), win_v_s.reshape(kv_shape),
            vn_tail[None], vn_first[0].reshape(DEPTH, dec_batch, dec_seq, A_WIDTH))
```

```python
import functools

import numpy as np
import jax
import jax.numpy as jnp
from jax import lax
from jax.experimental import pallas as pl
from jax.experimental.pallas import tpu as pltpu

F32 = jnp.float32
BF16 = jnp.bfloat16

D_MODEL = 1024
DEPTH = 1
PAST_LEN = 16384
N_HEADS = 16
N_KV_HEADS = 2
HEAD_DIM = 64
GROUP = N_HEADS // N_KV_HEADS
WINDOW = 128
ROPE_THETA = 10000.0
CHUNK = 128
A_GROUPS = 4
A_GROUP_DIM = 128
A_WIDTH = A_GROUPS * A_GROUP_DIM
D_FF = 2816
LN_EPS = 1e-5
NEG_INF = -1e30
DN_ALPHA = (2.0 * DEPTH) ** 0.25
LOG2E = 1.4426950408889634
Q_WIDTH = N_HEADS * HEAD_DIM
KV_WIDTH = N_KV_HEADS * HEAD_DIM
MIX_WIDTH = 2 * A_WIDTH + Q_WIDTH + 2 * KV_WIDTH
IN_WIDTH = MIX_WIDTH + 2 * D_MODEL

LANES = 128
BF16_TILE_ROWS = 16
VMEM_LIMIT_BYTES = 56 * 1024 * 1024

ROW_TILE = 512
FF_CHUNK = 256
PROJ_GROUP = 256
SAMPLE_KEY_PAD = 16
SAMPLE_BATCH_TILE = 16


def _resident(shape):
    nd = len(shape)
    return pl.BlockSpec(shape, lambda *_: (0,) * nd, pipeline_mode=pl.Buffered(1))


def _row_spec(tm, width):
    return pl.BlockSpec((tm, width), lambda i: (i, 0))


def _stream_specs(tm, width):
    return (pl.BlockSpec((tm, width), lambda i: (0, 0)),
            pl.BlockSpec((tm, width), lambda i: (jnp.maximum(i - 1, 0), 0)))


def _load_rows(refs):
    if len(refs) == 1:
        return refs[0][...]
    return jnp.where(pl.program_id(0) == 0, refs[0][...], refs[1][...])


def _store_rows(refs, value):
    if len(refs) == 1:
        refs[0][...] = value.astype(refs[0].dtype)
        return
    sample_ref, prompt_ref = refs
    prompt_ref[...] = value.astype(prompt_ref.dtype)

    @pl.when(pl.program_id(0) == 0)
    def _():
        sample_ref[...] = prompt_ref[...]


def _layernorm(y, g, b):
    mu = jnp.mean(y, axis=-1, keepdims=True)
    d = y - mu
    var = jnp.mean(d * d, axis=-1, keepdims=True)
    return d * lax.rsqrt(var + LN_EPS) * g + b


def _ffn_ln_rows(x, wup_ref, wdn_ref, g_ref, b_ref):
    xb = x.astype(BF16)
    y = None
    for c in range(D_FF // FF_CHUNK):
        lo = c * FF_CHUNK
        gate = jnp.dot(xb, wup_ref[:, lo:lo + FF_CHUNK], preferred_element_type=F32)
        up = jnp.dot(xb, wup_ref[:, D_FF + lo:D_FF + lo + FF_CHUNK], preferred_element_type=F32)
        act = (gate * jax.nn.sigmoid(gate) * up).astype(BF16)
        part = jnp.dot(act, wdn_ref[lo:lo + FF_CHUNK, :], preferred_element_type=F32)
        y = part if y is None else y + part
    return _layernorm(DN_ALPHA * x + 0.5 * y, g_ref[...], b_ref[...])


def _ffn_ln_kernel(*refs, n_in, n_out, cast_cols):
    x_refs, (wup_ref, wdn_ref, g_ref, b_ref), rest = refs[:n_in], refs[n_in:n_in + 4], refs[n_in + 4:]
    n_cast = (len(rest) - n_out) // 2
    cast_in, o_refs, cast_out = rest[:n_cast], rest[n_cast:n_cast + n_out], rest[n_cast + n_out:]
    x = _load_rows(x_refs).astype(F32)
    _store_rows(o_refs, _ffn_ln_rows(x, wup_ref, wdn_ref, g_ref, b_ref))
    for src, dst, (lo, hi) in zip(cast_in, cast_out, cast_cols):
        dst[...] = src[:, lo:hi].astype(dst.dtype)


def _cast_chunk_rows(rows, steps):
    return next(c for c in range(BF16_TILE_ROWS, rows + 1, BF16_TILE_ROWS)
                if rows % c == 0 and rows // c <= steps)


def _ffn_ln(xs, w_up, w_down, g, b, *, tm, out_rows, cast=()):
    xs = xs if isinstance(xs, tuple) else (xs,)
    outs = out_rows if isinstance(out_rows, tuple) else (out_rows,)
    steps = sum(outs) // tm
    pick = lambda n: _stream_specs(tm, D_MODEL) if n == 2 else (_row_spec(tm, D_MODEL),)
    cast_in_specs, cast_out_specs = [], []
    for w, lo, hi in cast:
        chunk = _cast_chunk_rows(w.shape[0], steps)
        rows = lambda i, last=w.shape[0] // chunk - 1: (jnp.minimum(i, last), 0)
        cast_in_specs.append(pl.BlockSpec((chunk, w.shape[1]), rows))
        cast_out_specs.append(pl.BlockSpec((chunk, hi - lo), rows))
    out = pl.pallas_call(
        functools.partial(_ffn_ln_kernel, n_in=len(xs), n_out=len(outs),
                          cast_cols=tuple((lo, hi) for _, lo, hi in cast)),
        grid=(steps,),
        in_specs=[*pick(len(xs)), _resident(w_up.shape), _resident(w_down.shape),
                  _resident(g.shape), _resident(b.shape)] + cast_in_specs,
        out_specs=[*pick(len(outs))] + cast_out_specs,
        out_shape=[jax.ShapeDtypeStruct((r, D_MODEL), F32) for r in outs]
        + [jax.ShapeDtypeStruct((w.shape[0], hi - lo), BF16) for w, lo, hi in cast],
        compiler_params=pltpu.CompilerParams(
            dimension_semantics=("arbitrary",), vmem_limit_bytes=VMEM_LIMIT_BYTES),
        name="ffn_ln",
    )(*xs, w_up, w_down, g, b, *[w for w, _, _ in cast])
    return tuple(out[:len(outs)]), tuple(out[len(outs):])


def _rope(x, cos, sin_signed, first_half):
    partner = jnp.where(first_half, pltpu.roll(x, LANES - HEAD_DIM // 2, axis=1),
                        pltpu.roll(x, HEAD_DIM // 2, axis=1))
    return x * cos + partner * sin_signed


def _spatial_mix(mix_ref, bias_ref, chunk_tokens):
    r = lax.broadcasted_iota(jnp.int32, (CHUNK, CHUNK), 0)
    c = lax.broadcasted_iota(jnp.int32, (CHUNK, CHUNK), 1)
    if chunk_tokens == CHUNK:
        return tuple(jnp.where(c <= r, mix_ref[g], 0.0).astype(BF16) for g in range(A_GROUPS)), bias_ref[...]
    token_mask = chunk_tokens - 1
    pick = jnp.where(((r & token_mask) == c) & (c < chunk_tokens), 1.0, 0.0).astype(BF16)
    keep = (c <= r) & ((c | token_mask) == (r | token_mask))
    mixes = []
    for g in range(A_GROUPS):
        rows_of_token = jnp.dot(pick, mix_ref[g].astype(BF16), preferred_element_type=F32).astype(BF16)
        tiled = lax.dot_general(rows_of_token, pick, (((1,), (1,)), ((), ())), preferred_element_type=F32)
        mixes.append(jnp.where(keep, tiled, 0.0).astype(BF16))
    token = lax.broadcasted_iota(jnp.int32, (CHUNK, A_WIDTH), 0) & token_mask
    bias = jnp.zeros((CHUNK, A_WIDTH), F32)
    for t in range(chunk_tokens):
        bias = jnp.where(token == t, bias_ref[t:t + 1, :], bias)
    return tuple(mixes), bias


def _in_proj_kernel(h_ref, w_ref, lng_ref, lnb_ref, mix_ref, bias_ref, rope_ref,
                    a_ref, q_ref, k_ref, v_ref, vn_first_ref, vn_tail_ref, k_tail_ref, v_tail_ref, *, sample_tokens):
    mixes, bias = lax.cond(pl.program_id(0) == 0,
                           lambda: _spatial_mix(mix_ref, bias_ref, sample_tokens),
                           lambda: _spatial_mix(mix_ref, bias_ref, CHUNK))
    hb = h_ref[...].astype(BF16)
    cos_k, sin_k, cos_q, sin_q = [rope_ref[:, i * LANES:(i + 1) * LANES] for i in range(4)]
    lane = lax.broadcasted_iota(jnp.int32, cos_k.shape, 1)
    first_half = (lane & (HEAD_DIM - 1)) < (HEAD_DIM // 2)

    def project(lo):
        return jnp.dot(hb, w_ref[:, lo:lo + PROJ_GROUP], preferred_element_type=F32)

    def rotary_q(i):
        x = project(2 * A_WIDTH + i * PROJ_GROUP)
        for j in range(PROJ_GROUP // LANES):
            cols = slice(i * PROJ_GROUP + j * LANES, i * PROJ_GROUP + (j + 1) * LANES)
            q_ref[:, cols] = _rope(x[:, j * LANES:(j + 1) * LANES], cos_q, sin_q, first_half).astype(q_ref.dtype)

    gu, gv = [], []
    for i in range(A_WIDTH // PROJ_GROUP):
        gu.append(jax.nn.gelu(project(i * PROJ_GROUP)))
        rotary_q(2 * i)
        gv.append(jax.nn.gelu(project(A_WIDTH + i * PROJ_GROUP)))
        rotary_q(2 * i + 1)
    kv = project(2 * A_WIDTH + Q_WIDTH)
    tail = slice(h_ref.shape[0] - WINDOW, h_ref.shape[0])
    k_ref[...] = _rope(kv[:, :KV_WIDTH], cos_k, sin_k, first_half)
    v_ref[...] = kv[:, KV_WIDTH:]
    k_tail_ref[0] = k_ref[tail, :]
    v_tail_ref[0] = v_ref[tail, :]

    gu = jnp.concatenate(gu, axis=1)
    vn = _layernorm(jnp.concatenate(gv, axis=1), lng_ref[...], lnb_ref[...])
    vn_tail_ref[0] = vn[tail, :]
    vn_first_ref[0] = vn
    vnb = vn.astype(BF16)
    for g in range(A_GROUPS):
        cols = slice(g * A_GROUP_DIM, (g + 1) * A_GROUP_DIM)
        for ch in range(h_ref.shape[0] // CHUNK):
            rows = slice(ch * CHUNK, (ch + 1) * CHUNK)
            mixed = jnp.dot(mixes[g], vnb[rows, cols], preferred_element_type=F32) + bias[:, cols]
            a_ref[rows, cols] = (gu[rows, cols] * mixed).astype(a_ref.dtype)


def _in_proj(h, w_in, a_ln_g, a_ln_b, mix, bias_full, rope, *, tm, table_blocks, sample_tokens):
    t = h.shape[0]
    sequences = (t // tm - 1) // table_blocks
    table = pl.BlockSpec((tm, 4 * LANES), lambda i: (jnp.where(i == 0, 0, 1 + (i - 1) % table_blocks), 0))
    tail = lambda width: pl.BlockSpec((1, WINDOW, width), lambda i: (jnp.maximum(i - 1, 0) // table_blocks, 0, 0))
    return pl.pallas_call(
        functools.partial(_in_proj_kernel, sample_tokens=sample_tokens),
        grid=(t // tm,),
        in_specs=[_row_spec(tm, D_MODEL), _resident(w_in.shape), _resident(a_ln_g.shape), _resident(a_ln_b.shape),
                  _resident(mix.shape), _resident(bias_full.shape), table],
        out_specs=[_row_spec(tm, A_WIDTH), _row_spec(tm, Q_WIDTH), _row_spec(tm, KV_WIDTH), _row_spec(tm, KV_WIDTH),
                   pl.BlockSpec((1, tm, A_WIDTH), lambda i: (jnp.minimum(i, 1), 0, 0)),
                   tail(A_WIDTH), tail(KV_WIDTH), tail(KV_WIDTH)],
        out_shape=[jax.ShapeDtypeStruct((t, A_WIDTH), BF16), jax.ShapeDtypeStruct((t, Q_WIDTH), BF16),
                   jax.ShapeDtypeStruct((t, KV_WIDTH), F32), jax.ShapeDtypeStruct((t, KV_WIDTH), F32),
                   jax.ShapeDtypeStruct((2, tm, A_WIDTH), F32), jax.ShapeDtypeStruct((sequences, WINDOW, A_WIDTH), F32),
                   jax.ShapeDtypeStruct((sequences, WINDOW, KV_WIDTH), F32),
                   jax.ShapeDtypeStruct((sequences, WINDOW, KV_WIDTH), F32)],
        compiler_params=pltpu.CompilerParams(
            dimension_semantics=("arbitrary",), vmem_limit_bytes=VMEM_LIMIT_BYTES),
        name="in_proj",
    )(h, w_in, a_ln_g, a_ln_b, mix, bias_full, rope)


def _kv_tiles(k2, v2, kv_head):
    low = lax.broadcasted_iota(jnp.int32, k2.shape, 1) < HEAD_DIM
    k2r = pltpu.roll(k2, HEAD_DIM, axis=1)
    v2r = pltpu.roll(v2, HEAD_DIM, axis=1)
    if kv_head == 0:
        keys, vals = jnp.where(low, k2, k2r), jnp.where(low, v2, v2r)
    else:
        keys, vals = jnp.where(low, k2r, k2), jnp.where(low, v2r, v2)
    vals = jnp.concatenate([vals, jnp.ones_like(vals)], axis=1)
    return keys.astype(BF16), vals.astype(BF16)


def _kv_attention(q_pieces, sink_pieces, keys, vals, bias):
    n, rows = len(q_pieces), q_pieces[0].shape[0]
    q = jnp.concatenate(q_pieces, axis=0) if n > 1 else q_pieces[0]
    q_low = lax.broadcasted_iota(jnp.int32, q.shape, 1) < HEAD_DIM
    zero = jnp.zeros_like(q)
    q_all = jnp.concatenate([jnp.where(q_low, q, zero), jnp.where(q_low, zero, q)], axis=0)
    s_all = lax.dot_general(q_all, keys, (((1,), (1,)), ((), ())), preferred_element_type=F32)
    probs, sink_terms = [], []
    for parity in range(2):
        for piece in range(n):
            lo = (parity * n + piece) * rows
            s = s_all[lo:lo + rows] + bias
            m = jnp.max(s, axis=-1, keepdims=True)
            probs.append(jnp.exp2(s - m).astype(BF16))
            sink_terms.append(jnp.exp2(sink_pieces[parity][piece] - m))
    o_all = jnp.dot(jnp.concatenate(probs, axis=0), vals, preferred_element_type=F32)
    o_low = lax.broadcasted_iota(jnp.int32, (rows, LANES), 1) < HEAD_DIM
    outs = []
    for piece in range(n):
        even = o_all[piece * rows:(piece + 1) * rows]
        odd = o_all[(n + piece) * rows:(n + piece + 1) * rows]
        num = jnp.where(o_low, even[:, :LANES], odd[:, :LANES])
        den = jnp.where(o_low, even[:, LANES:] + sink_terms[piece], odd[:, LANES:] + sink_terms[n + piece])
        outs.append(num / den)
    return outs


def _attend_block(blk, sink_ref, bias, q_ref, k_prev, v_prev, k_ref, v_ref, o_ref):
    slabs = GROUP // 2
    rows = slice(blk * WINDOW, (blk + 1) * WINDOW)
    k2 = jnp.concatenate([k_prev, k_ref[rows, :]], axis=0)
    v2 = jnp.concatenate([v_prev, v_ref[rows, :]], axis=0)
    for kh in range(N_KV_HEADS):
        keys, vals = _kv_tiles(k2, v2, kh)
        cols = [slice((kh * slabs + s) * LANES, (kh * slabs + s + 1) * LANES) for s in range(slabs)]
        sinks = [[sink_ref[kh * GROUP + 2 * s + parity] * LOG2E for s in range(slabs)] for parity in range(2)]
        outs = _kv_attention([q_ref[rows, c] for c in cols], sinks, keys, vals, bias)
        for c, o in zip(cols, outs):
            o_ref[rows, c] = o.astype(o_ref.dtype)


def _band_bias(query_pos, first_block):
    i = np.asarray(query_pos)[:, None]
    j = np.arange(2 * WINDOW)[None, :]
    prev_ok = (j < WINDOW) & (j > i) & (not first_block)
    own_ok = (j >= WINDOW) & (j - WINDOW <= i)
    return np.where(prev_ok | own_ok, 0.0, NEG_INF).astype(np.float32)


def _attn_sample_kernel(sink_ref, bias_ref, q_ref, kn_ref, vn_ref, kc_ref, vc_ref, o_ref, wk_ref, wv_ref, *,
                        dec_seq):
    rows = (GROUP // 2) * dec_seq
    key_pad = jnp.zeros((WINDOW - SAMPLE_KEY_PAD, KV_WIDTH), F32)
    bias = bias_ref[...]
    slab = lax.broadcasted_iota(jnp.int32, (rows, 1), 0) // dec_seq
    sinks = []
    for kh in range(N_KV_HEADS):
        per_parity = []
        for parity in range(2):
            sink = jnp.zeros((rows, 1), F32)
            for s in range(GROUP // 2):
                sink = jnp.where(slab == s, sink_ref[kh * GROUP + 2 * s + parity] * LOG2E, sink)
            per_parity.append([sink])
        sinks.append(per_parity)
    for b in range(q_ref.shape[0]):
        kc, vc, kn, vn = kc_ref[b], vc_ref[b], kn_ref[b], vn_ref[b]
        wk_ref[b, 0:WINDOW - dec_seq, :] = kc[dec_seq:, :]
        wk_ref[b, WINDOW - dec_seq:WINDOW, :] = kn[0:dec_seq, :]
        wv_ref[b, 0:WINDOW - dec_seq, :] = vc[dec_seq:, :]
        wv_ref[b, WINDOW - dec_seq:WINDOW, :] = vn[0:dec_seq, :]
        k2 = jnp.concatenate([kc, kn, key_pad], axis=0)
        v2 = jnp.concatenate([vc, vn, key_pad], axis=0)
        for kh in range(N_KV_HEADS):
            keys, vals = _kv_tiles(k2, v2, kh)
            q_rows = q_ref[b, kh * rows:(kh + 1) * rows, :]
            (out,) = _kv_attention([q_rows], sinks[kh], keys, vals, bias)
            o_ref[b, kh * rows:(kh + 1) * rows, :] = out.astype(o_ref.dtype)


def _attn_sample(sinks, q, k_new, v_new, k_cache, v_cache, *, dec_seq):
    nb = q.shape[0]
    bt = SAMPLE_BATCH_TILE
    blk = lambda rows: pl.BlockSpec((bt, rows, LANES), lambda i: (i, 0, 0))
    q_rows = N_KV_HEADS * (GROUP // 2) * dec_seq
    bias = jnp.asarray(_band_bias(np.arange((GROUP // 2) * dec_seq) % dec_seq, False))
    return pl.pallas_call(
        functools.partial(_attn_sample_kernel, dec_seq=dec_seq),
        grid=(nb // bt,),
        in_specs=[pl.BlockSpec(memory_space=pltpu.SMEM), _resident(bias.shape), blk(q_rows),
                  blk(SAMPLE_KEY_PAD), blk(SAMPLE_KEY_PAD), blk(WINDOW), blk(WINDOW)],
        out_specs=[blk(q_rows), blk(WINDOW), blk(WINDOW)],
        out_shape=[jax.ShapeDtypeStruct((nb, q_rows, LANES), BF16),
                   jax.ShapeDtypeStruct((nb, WINDOW, KV_WIDTH), F32),
                   jax.ShapeDtypeStruct((nb, WINDOW, KV_WIDTH), F32)],
        compiler_params=pltpu.CompilerParams(
            dimension_semantics=("arbitrary",), vmem_limit_bytes=VMEM_LIMIT_BYTES),
        name="attn_sample",
    )(sinks, bias, q, k_new, v_new, k_cache, v_cache)


def _mix_ffn_kernel(sink_ref, bias_ref, h_ref, a_ref, q_ref, kp_ref, k_ref, vp_ref, v_ref, bs_ref,
                    wg_ref, wpa_ref, wpb_ref, wo_ref, g2_ref, b2_ref, wup_ref, wdn_ref, g3_ref, b3_ref,
                    ys_ref, yp_ref, b_buf, *, tiles_per_seq):
    step = pl.program_id(0)
    first_of_sequence = (step - 1) % tiles_per_seq == 0
    h = h_ref[...]
    hb = h.astype(BF16)
    a = a_ref[...]
    gated_a, gates_b = [], []
    for blk in range(h_ref.shape[0] // WINDOW):
        if blk == 0:
            bias = bias_ref[jnp.where(first_of_sequence, 1, 0)]
            k_prev, v_prev = kp_ref[...], vp_ref[...]
        else:
            bias = bias_ref[0]
            k_prev, v_prev = k_ref[(blk - 1) * WINDOW:blk * WINDOW, :], v_ref[(blk - 1) * WINDOW:blk * WINDOW, :]
        _attend_block(blk, sink_ref, bias, q_ref, k_prev, v_prev, k_ref, v_ref, b_buf)
        cols = slice(blk * PROJ_GROUP, (blk + 1) * PROJ_GROUP)
        gate_b_cols = slice(D_MODEL + cols.start, D_MODEL + cols.stop)
        gate_a = jax.nn.sigmoid(jnp.dot(hb, wg_ref[:, cols], preferred_element_type=F32))
        gated_a.append(gate_a * jnp.dot(a, wpa_ref[:, cols], preferred_element_type=F32))
        gates_b.append(jax.nn.sigmoid(jnp.dot(hb, wg_ref[:, gate_b_cols], preferred_element_type=F32)))
    b = jnp.where(step == 0, bs_ref[...], b_buf[...])
    merged = []
    for blk, (part, gate_b) in enumerate(zip(gated_a, gates_b)):
        cols = slice(blk * PROJ_GROUP, (blk + 1) * PROJ_GROUP)
        part = part + gate_b * jnp.dot(b, wpb_ref[:, cols], preferred_element_type=F32)
        merged.append(part.astype(BF16))
    y = jnp.dot(jnp.concatenate(merged, axis=1), wo_ref[...], preferred_element_type=F32)
    h2 = _layernorm(DN_ALPHA * h + y, g2_ref[...], b2_ref[...])
    _store_rows((ys_ref, yp_ref), _ffn_ln_rows(h2, wup_ref, wdn_ref, g3_ref, b3_ref))


def _mix_ffn(sinks, h, a_out, q, k, v, b_sample, w_gate, w_pa, w_pb, w_o, g2, b2, w_up, w_down, g3, b3, *, tm,
             tiles_per_seq):
    t = h.shape[0]
    assert tm // WINDOW == D_MODEL // PROJ_GROUP
    bias = jnp.asarray(np.stack([_band_bias(np.arange(WINDOW), False), _band_bias(np.arange(WINDOW), True)]))
    prev = pl.BlockSpec((WINDOW, KV_WIDTH), lambda i: (jnp.maximum(i * (tm // WINDOW) - 1, 0), 0))
    return pl.pallas_call(
        functools.partial(_mix_ffn_kernel, tiles_per_seq=tiles_per_seq),
        grid=(t // tm,),
        in_specs=[pl.BlockSpec(memory_space=pltpu.SMEM), _resident(bias.shape),
                  _row_spec(tm, D_MODEL), _row_spec(tm, A_WIDTH), _row_spec(tm, Q_WIDTH),
                  prev, _row_spec(tm, KV_WIDTH), prev, _row_spec(tm, KV_WIDTH),
                  pl.BlockSpec((tm, Q_WIDTH), lambda i: (0, 0), pipeline_mode=pl.Buffered(1)),
                  _resident(w_gate.shape), _resident(w_pa.shape), _resident(w_pb.shape), _resident(w_o.shape),
                  _resident(g2.shape), _resident(b2.shape), _resident(w_up.shape), _resident(w_down.shape),
                  _resident(g3.shape), _resident(b3.shape)],
        out_specs=[*_stream_specs(tm, D_MODEL)],
        out_shape=[jax.ShapeDtypeStruct((tm, D_MODEL), F32), jax.ShapeDtypeStruct((t - tm, D_MODEL), F32)],
        scratch_shapes=[pltpu.VMEM((tm, Q_WIDTH), BF16)],
        compiler_params=pltpu.CompilerParams(
            dimension_semantics=("arbitrary",), vmem_limit_bytes=VMEM_LIMIT_BYTES),
        name="mix_ffn",
    )(sinks, bias, h, a_out, q, k, k, v, v, b_sample, w_gate, w_pa, w_pb, w_o, g2, b2, w_up, w_down, g3, b3)


def _rope_tables(positions):
    half = HEAD_DIM // 2
    inv = ROPE_THETA ** (-np.arange(half, dtype=np.float64) / half)
    ang = np.asarray(positions, np.float64)[:, None] * inv[None, :]
    cos = np.tile(np.cos(ang), (1, 2 * LANES // HEAD_DIM))
    sin = np.tile(np.concatenate([-np.sin(ang), np.sin(ang)], axis=1), (1, LANES // HEAD_DIM))
    q_scale = LOG2E * HEAD_DIM ** -0.5
    return jnp.asarray(np.concatenate([cos, sin, cos * q_scale, sin * q_scale], axis=1), F32)


def kernel(x_prompt, x_sample, cache_win_k, cache_win_v, ffn1_up, ffn1_down, ln1_g, ln1_b, w_in, a_ln_g, a_ln_b,
           a_ws, a_bs, attn_sinks, w_pa, w_pb, w_o, ln2_g, ln2_b, ffn2_up, ffn2_down, ln3_g, ln3_b):
    batch, seq, _ = x_prompt.shape
    dec_batch, dec_seq, _ = x_sample.shape
    buf = cache_win_k.shape[2]
    assert DEPTH == 1 and buf == WINDOW and seq % ROW_TILE == 0 and ROW_TILE % CHUNK == 0
    assert CHUNK % dec_seq == 0 and (dec_batch * dec_seq) % CHUNK == 0 and dec_batch % SAMPLE_BATCH_TILE == 0
    assert dec_seq & (dec_seq - 1) == 0
    tp, ts = batch * seq, dec_batch * dec_seq
    assert ts == ROW_TILE
    assert seq % CHUNK == 0 and CHUNK == WINDOW

    up1, dn1 = ffn1_up[0].astype(BF16), ffn1_down[0].astype(BF16)
    sinks = attn_sinks[0]
    mix = a_ws[0]
    mix_bias = jnp.repeat(a_bs[0].T, A_GROUP_DIM, axis=1)
    rope = _rope_tables(np.concatenate([PAST_LEN + np.arange(ts) % dec_seq, np.arange(seq)]))

    xs, xp = x_sample.reshape(ts, D_MODEL), x_prompt.reshape(tp, D_MODEL)
    whole = lambda w: (w, 0, w.shape[1])
    (h,), (up2, dn2, w_mix, w_gate, wpa, wpb, wo) = _ffn_ln(
        (xs, xp), up1, dn1, ln1_g, ln1_b, tm=ROW_TILE, out_rows=ts + tp,
        cast=(whole(ffn2_up[0]), whole(ffn2_down[0]), (w_in[0], 0, MIX_WIDTH), (w_in[0], MIX_WIDTH, IN_WIDTH),
              whole(w_pa[0]), whole(w_pb[0]), whole(w_o[0])))
    a_out, q, k, v, vn_first, vn_tail, k_tail, v_tail = _in_proj(
        h, w_mix, a_ln_g, a_ln_b, mix, mix_bias, rope, tm=ROW_TILE, table_blocks=seq // ROW_TILE,
        sample_tokens=dec_seq)

    slabs = Q_WIDTH // LANES
    qs_t = q[:ts].reshape(dec_batch, dec_seq, slabs, LANES).transpose(0, 2, 1, 3).reshape(
        dec_batch, slabs * dec_seq, LANES)
    pad = ((0, 0), (0, SAMPLE_KEY_PAD - dec_seq), (0, 0))
    ks_pad = jnp.pad(k[:ts].reshape(dec_batch, dec_seq, KV_WIDTH), pad)
    vs_pad = jnp.pad(v[:ts].reshape(dec_batch, dec_seq, KV_WIDTH), pad)
    bs_t, win_k_s, win_v_s = _attn_sample(
        sinks, qs_t, ks_pad, vs_pad, cache_win_k[0].reshape(dec_batch, buf, KV_WIDTH),
        cache_win_v[0].reshape(dec_batch, buf, KV_WIDTH), dec_seq=dec_seq)
    b_sample = bs_t.reshape(dec_batch, slabs, dec_seq, LANES).transpose(0, 2, 1, 3).reshape(ts, Q_WIDTH)

    ys, yp = _mix_ffn(sinks, h, a_out, q, k, v, b_sample, w_gate, wpa, wpb, wo, ln2_g, ln2_b, up2, dn2, ln3_g, ln3_b,
                      tm=ROW_TILE, tiles_per_seq=seq // ROW_TILE)

    kv_shape = (DEPTH, -1, buf, N_KV_HEADS, HEAD_DIM)
    return (yp.reshape(batch, seq, D_MODEL), ys.reshape(dec_batch, dec_seq, D_MODEL),
            k_tail.reshape(kv_shape), v_tail.reshape(kv_shape),
            win_k_s.reshape(kv_shape), win_v_s.reshape(kv_shape),
            vn_tail[None], vn_first[0].reshape(DEPTH, dec_batch, dec_seq, A_WIDTH))
```

```python
import functools

import numpy as np
import jax
import jax.numpy as jnp
from jax import lax
from jax.experimental import pallas as pl
from jax.experimental.pallas import tpu as pltpu

F32 = jnp.float32
BF16 = jnp.bfloat16

D_MODEL = 1024
DEPTH = 1
PAST_LEN = 16384
N_HEADS = 16
N_KV_HEADS = 2
HEAD_DIM = 64
GROUP = N_HEADS // N_KV_HEADS
WINDOW = 128
ROPE_THETA = 10000.0
CHUNK = 128
A_GROUPS = 4
A_GROUP_DIM = 128
A_WIDTH = A_GROUPS * A_GROUP_DIM
D_FF = 2816
LN_EPS = 1e-5
NEG_INF = -1e30
DN_ALPHA = (2.0 * DEPTH) ** 0.25
LOG2E = 1.4426950408889634
Q_WIDTH = N_HEADS * HEAD_DIM
KV_WIDTH = N_KV_HEADS * HEAD_DIM
MIX_WIDTH = 2 * A_WIDTH + Q_WIDTH + 2 * KV_WIDTH
IN_WIDTH = MIX_WIDTH + 2 * D_MODEL

LANES = 128
BF16_TILE_ROWS = 16
VMEM_LIMIT_BYTES = 56 * 1024 * 1024

ROW_TILE = 512
FF_CHUNK = 256
PROJ_GROUP = 256
SAMPLE_KEY_PAD = 16
SAMPLE_BATCH_TILE = 16


def _resident(shape):
    nd = len(shape)
    return pl.BlockSpec(shape, lambda *_: (0,) * nd, pipeline_mode=pl.Buffered(1))


def _row_spec(tm, width):
    return pl.BlockSpec((tm, width), lambda i: (i, 0))


def _stream_specs(tm, width):
    return (pl.BlockSpec((tm, width), lambda i: (0, 0)),
            pl.BlockSpec((tm, width), lambda i: (jnp.maximum(i - 1, 0), 0)))


def _load_rows(refs):
    if len(refs) == 1:
        return refs[0][...]
    return jnp.where(pl.program_id(0) == 0, refs[0][...], refs[1][...])


def _store_rows(refs, value):
    if len(refs) == 1:
        refs[0][...] = value.astype(refs[0].dtype)
        return
    sample_ref, prompt_ref = refs
    prompt_ref[...] = value.astype(prompt_ref.dtype)

    @pl.when(pl.program_id(0) == 0)
    def _():
        sample_ref[...] = prompt_ref[...]


def _layernorm(y, g, b):
    mu = jnp.mean(y, axis=-1, keepdims=True)
    d = y - mu
    var = jnp.mean(d * d, axis=-1, keepdims=True)
    return d * lax.rsqrt(var + LN_EPS) * g + b


def _ffn_ln_rows(x, wup_ref, wdn_ref, g_ref, b_ref):
    xb = x.astype(BF16)
    y = None
    for c in range(D_FF // FF_CHUNK):
        lo = c * FF_CHUNK
        gate = jnp.dot(xb, wup_ref[:, lo:lo + FF_CHUNK], preferred_element_type=F32)
        up = jnp.dot(xb, wup_ref[:, D_FF + lo:D_FF + lo + FF_CHUNK], preferred_element_type=F32)
        act = (gate * jax.nn.sigmoid(gate) * up).astype(BF16)
        part = jnp.dot(act, wdn_ref[lo:lo + FF_CHUNK, :], preferred_element_type=F32)
        y = part if y is None else y + part
    return _layernorm(DN_ALPHA * x + 0.5 * y, g_ref[...], b_ref[...])


def _ffn_ln_kernel(*refs, n_in, n_out, cast_cols, n_transpose):
    x_refs, (wup_ref, wdn_ref, g_ref, b_ref), rest = refs[:n_in], refs[n_in:n_in + 4], refs[n_in + 4:]
    n_side = len(cast_cols) + n_transpose
    side_in, o_refs, side_out = rest[:n_side], rest[n_side:n_side + n_out], rest[n_side + n_out:]
    x = _load_rows(x_refs).astype(F32)
    _store_rows(o_refs, _ffn_ln_rows(x, wup_ref, wdn_ref, g_ref, b_ref))
    for src, dst, (lo, hi) in zip(side_in, side_out, cast_cols):
        dst[...] = src[:, lo:hi].astype(dst.dtype)
    for src, dst in zip(side_in[len(cast_cols):], side_out[len(cast_cols):]):
        for j in range(src.shape[0]):
            dst[j] = src[j].T


def _cast_chunk_rows(rows, steps):
    return next(c for c in range(BF16_TILE_ROWS, rows + 1, BF16_TILE_ROWS)
                if rows % c == 0 and rows // c <= steps)


def _ffn_ln(xs, w_up, w_down, g, b, *, tm, out_rows, cast=(), transpose=()):
    xs = xs if isinstance(xs, tuple) else (xs,)
    outs = out_rows if isinstance(out_rows, tuple) else (out_rows,)
    steps = sum(outs) // tm
    pick = lambda n: _stream_specs(tm, D_MODEL) if n == 2 else (_row_spec(tm, D_MODEL),)
    side_in_specs, side_out_specs, side_shapes = [], [], []
    for w, lo, hi in cast:
        chunk = _cast_chunk_rows(w.shape[0], steps)
        rows = lambda i, last=w.shape[0] // chunk - 1: (jnp.minimum(i, last), 0)
        side_in_specs.append(pl.BlockSpec((chunk, w.shape[1]), rows))
        side_out_specs.append(pl.BlockSpec((chunk, hi - lo), rows))
        side_shapes.append(jax.ShapeDtypeStruct((w.shape[0], hi - lo), BF16))
    for t in transpose:
        batches, r, c = t.shape
        per_step = next(n for n in range(1, batches + 1) if batches % n == 0 and batches // n <= steps)
        index = lambda i, last=batches // per_step - 1: (jnp.minimum(i, last), 0, 0)
        side_in_specs.append(pl.BlockSpec((per_step, r, c), index))
        side_out_specs.append(pl.BlockSpec((per_step, c, r), index))
        side_shapes.append(jax.ShapeDtypeStruct((batches, c, r), t.dtype))
    out = pl.pallas_call(
        functools.partial(_ffn_ln_kernel, n_in=len(xs), n_out=len(outs),
                          cast_cols=tuple((lo, hi) for _, lo, hi in cast), n_transpose=len(transpose)),
        grid=(steps,),
        in_specs=[*pick(len(xs)), _resident(w_up.shape), _resident(w_down.shape),
                  _resident(g.shape), _resident(b.shape)] + side_in_specs,
        out_specs=[*pick(len(outs))] + side_out_specs,
        out_shape=[jax.ShapeDtypeStruct((r, D_MODEL), F32) for r in outs] + side_shapes,
        compiler_params=pltpu.CompilerParams(
            dimension_semantics=("arbitrary",), vmem_limit_bytes=VMEM_LIMIT_BYTES),
        name="ffn_ln",
    )(*xs, w_up, w_down, g, b, *[w for w, _, _ in cast], *transpose)
    return tuple(out[:len(outs)]), tuple(out[len(outs):])


def _rope(x, cos, sin_signed, first_half):
    partner = jnp.where(first_half, pltpu.roll(x, LANES - HEAD_DIM // 2, axis=1),
                        pltpu.roll(x, HEAD_DIM // 2, axis=1))
    return x * cos + partner * sin_signed


def _spatial_mix(mix_ref, bias_ref, chunk_tokens):
    r = lax.broadcasted_iota(jnp.int32, (CHUNK, CHUNK), 0)
    c = lax.broadcasted_iota(jnp.int32, (CHUNK, CHUNK), 1)
    if chunk_tokens == CHUNK:
        return tuple(jnp.where(c <= r, mix_ref[g], 0.0).astype(BF16) for g in range(A_GROUPS)), bias_ref[...]
    token_mask = chunk_tokens - 1
    pick = jnp.where(((r & token_mask) == c) & (c < chunk_tokens), 1.0, 0.0).astype(BF16)
    keep = (c <= r) & ((c | token_mask) == (r | token_mask))
    mixes = []
    for g in range(A_GROUPS):
        rows_of_token = jnp.dot(pick, mix_ref[g].astype(BF16), preferred_element_type=F32).astype(BF16)
        tiled = lax.dot_general(rows_of_token, pick, (((1,), (1,)), ((), ())), preferred_element_type=F32)
        mixes.append(jnp.where(keep, tiled, 0.0).astype(BF16))
    token = lax.broadcasted_iota(jnp.int32, (CHUNK, A_WIDTH), 0) & token_mask
    bias = jnp.zeros((CHUNK, A_WIDTH), F32)
    for t in range(chunk_tokens):
        bias = jnp.where(token == t, bias_ref[t:t + 1, :], bias)
    return tuple(mixes), bias


def _in_proj_kernel(h_ref, w_ref, lng_ref, lnb_ref, mix_ref, bias_ref, rope_ref,
                    a_ref, q_ref, k_ref, v_ref, vn_first_ref, vn_tail_ref, k_tail_ref, v_tail_ref, *, sample_tokens):
    mixes, bias = lax.cond(pl.program_id(0) == 0,
                           lambda: _spatial_mix(mix_ref, bias_ref, sample_tokens),
                           lambda: _spatial_mix(mix_ref, bias_ref, CHUNK))
    hb = h_ref[...].astype(BF16)
    cos_k, sin_k, cos_q, sin_q = [rope_ref[:, i * LANES:(i + 1) * LANES] for i in range(4)]
    lane = lax.broadcasted_iota(jnp.int32, cos_k.shape, 1)
    first_half = (lane & (HEAD_DIM - 1)) < (HEAD_DIM // 2)

    def project(lo):
        return jnp.dot(hb, w_ref[:, lo:lo + PROJ_GROUP], preferred_element_type=F32)

    def rotary_q(i):
        x = project(2 * A_WIDTH + i * PROJ_GROUP)
        for j in range(PROJ_GROUP // LANES):
            cols = slice(i * PROJ_GROUP + j * LANES, i * PROJ_GROUP + (j + 1) * LANES)
            q_ref[:, cols] = _rope(x[:, j * LANES:(j + 1) * LANES], cos_q, sin_q, first_half).astype(q_ref.dtype)

    gu, gv = [], []
    for i in range(A_WIDTH // PROJ_GROUP):
        gu.append(jax.nn.gelu(project(i * PROJ_GROUP)))
        rotary_q(2 * i)
        gv.append(jax.nn.gelu(project(A_WIDTH + i * PROJ_GROUP)))
        rotary_q(2 * i + 1)
    kv = project(2 * A_WIDTH + Q_WIDTH)
    tail = slice(h_ref.shape[0] - WINDOW, h_ref.shape[0])
    k_ref[...] = _rope(kv[:, :KV_WIDTH], cos_k, sin_k, first_half)
    v_ref[...] = kv[:, KV_WIDTH:]
    k_tail_ref[0] = k_ref[tail, :]
    v_tail_ref[0] = v_ref[tail, :]

    gu = jnp.concatenate(gu, axis=1)
    vn = _layernorm(jnp.concatenate(gv, axis=1), lng_ref[...], lnb_ref[...])
    vn_tail_ref[0] = vn[tail, :]
    vn_first_ref[0] = vn
    vnb = vn.astype(BF16)
    for g in range(A_GROUPS):
        cols = slice(g * A_GROUP_DIM, (g + 1) * A_GROUP_DIM)
        for ch in range(h_ref.shape[0] // CHUNK):
            rows = slice(ch * CHUNK, (ch + 1) * CHUNK)
            mixed = jnp.dot(mixes[g], vnb[rows, cols], preferred_element_type=F32) + bias[:, cols]
            a_ref[rows, cols] = (gu[rows, cols] * mixed).astype(a_ref.dtype)


def _in_proj(h, w_in, a_ln_g, a_ln_b, mix, bias_full, rope, *, tm, table_blocks, sample_tokens):
    t = h.shape[0]
    sequences = (t // tm - 1) // table_blocks
    table = pl.BlockSpec((tm, 4 * LANES), lambda i: (jnp.where(i == 0, 0, 1 + (i - 1) % table_blocks), 0))
    tail = lambda width: pl.BlockSpec((1, WINDOW, width), lambda i: (jnp.maximum(i - 1, 0) // table_blocks, 0, 0))
    return pl.pallas_call(
        functools.partial(_in_proj_kernel, sample_tokens=sample_tokens),
        grid=(t // tm,),
        in_specs=[_row_spec(tm, D_MODEL), _resident(w_in.shape), _resident(a_ln_g.shape), _resident(a_ln_b.shape),
                  _resident(mix.shape), _resident(bias_full.shape), table],
        out_specs=[_row_spec(tm, A_WIDTH), _row_spec(tm, Q_WIDTH), _row_spec(tm, KV_WIDTH), _row_spec(tm, KV_WIDTH),
                   pl.BlockSpec((1, tm, A_WIDTH), lambda i: (jnp.minimum(i, 1), 0, 0)),
                   tail(A_WIDTH), tail(KV_WIDTH), tail(KV_WIDTH)],
        out_shape=[jax.ShapeDtypeStruct((t, A_WIDTH), BF16), jax.ShapeDtypeStruct((t, Q_WIDTH), BF16),
                   jax.ShapeDtypeStruct((t, KV_WIDTH), F32), jax.ShapeDtypeStruct((t, KV_WIDTH), F32),
                   jax.ShapeDtypeStruct((2, tm, A_WIDTH), F32), jax.ShapeDtypeStruct((sequences, WINDOW, A_WIDTH), F32),
                   jax.ShapeDtypeStruct((sequences, WINDOW, KV_WIDTH), F32),
                   jax.ShapeDtypeStruct((sequences, WINDOW, KV_WIDTH), F32)],
        compiler_params=pltpu.CompilerParams(
            dimension_semantics=("arbitrary",), vmem_limit_bytes=VMEM_LIMIT_BYTES),
        name="in_proj",
    )(h, w_in, a_ln_g, a_ln_b, mix, bias_full, rope)


def _kv_tiles(k2, v2, kv_head):
    low = lax.broadcasted_iota(jnp.int32, k2.shape, 1) < HEAD_DIM
    k2r = pltpu.roll(k2, HEAD_DIM, axis=1)
    v2r = pltpu.roll(v2, HEAD_DIM, axis=1)
    if kv_head == 0:
        keys, vals = jnp.where(low, k2, k2r), jnp.where(low, v2, v2r)
    else:
        keys, vals = jnp.where(low, k2r, k2), jnp.where(low, v2r, v2)
    vals = jnp.concatenate([vals, jnp.ones_like(vals)], axis=1)
    return keys.astype(BF16), vals.astype(BF16)


def _kv_attention(q_pieces, sink_pieces, keys, vals, bias):
    n, rows = len(q_pieces), q_pieces[0].shape[0]
    q = jnp.concatenate(q_pieces, axis=0) if n > 1 else q_pieces[0]
    q_low = lax.broadcasted_iota(jnp.int32, q.shape, 1) < HEAD_DIM
    zero = jnp.zeros_like(q)
    q_all = jnp.concatenate([jnp.where(q_low, q, zero), jnp.where(q_low, zero, q)], axis=0)
    s_all = lax.dot_general(q_all, keys, (((1,), (1,)), ((), ())), preferred_element_type=F32)
    probs, sink_terms = [], []
    for parity in range(2):
        for piece in range(n):
            lo = (parity * n + piece) * rows
            s = s_all[lo:lo + rows] + bias
            m = jnp.max(s, axis=-1, keepdims=True)
            probs.append(jnp.exp2(s - m).astype(BF16))
            sink_terms.append(jnp.exp2(sink_pieces[parity][piece] - m))
    o_all = jnp.dot(jnp.concatenate(probs, axis=0), vals, preferred_element_type=F32)
    o_low = lax.broadcasted_iota(jnp.int32, (rows, LANES), 1) < HEAD_DIM
    outs = []
    for piece in range(n):
        even = o_all[piece * rows:(piece + 1) * rows]
        odd = o_all[(n + piece) * rows:(n + piece + 1) * rows]
        num = jnp.where(o_low, even[:, :LANES], odd[:, :LANES])
        den = jnp.where(o_low, even[:, LANES:] + sink_terms[piece], odd[:, LANES:] + sink_terms[n + piece])
        outs.append(num / den)
    return outs


def _attend_block(blk, sink_ref, bias, q_ref, k_prev, v_prev, k_ref, v_ref, o_ref):
    slabs = GROUP // 2
    rows = slice(blk * WINDOW, (blk + 1) * WINDOW)
    k2 = jnp.concatenate([k_prev, k_ref[rows, :]], axis=0)
    v2 = jnp.concatenate([v_prev, v_ref[rows, :]], axis=0)
    for kh in range(N_KV_HEADS):
        keys, vals = _kv_tiles(k2, v2, kh)
        cols = [slice((kh * slabs + s) * LANES, (kh * slabs + s + 1) * LANES) for s in range(slabs)]
        sinks = [[sink_ref[kh * GROUP + 2 * s + parity] * LOG2E for s in range(slabs)] for parity in range(2)]
        outs = _kv_attention([q_ref[rows, c] for c in cols], sinks, keys, vals, bias)
        for c, o in zip(cols, outs):
            o_ref[rows, c] = o.astype(o_ref.dtype)


def _band_bias(query_pos, first_block):
    i = np.asarray(query_pos)[:, None]
    j = np.arange(2 * WINDOW)[None, :]
    prev_ok = (j < WINDOW) & (j > i) & (not first_block)
    own_ok = (j >= WINDOW) & (j - WINDOW <= i)
    return np.where(prev_ok | own_ok, 0.0, NEG_INF).astype(np.float32)


def _attn_sample_kernel(sink_ref, bias_ref, q_ref, kn_ref, vn_ref, kc_ref, vc_ref, o_ref, wk_ref, wv_ref, *,
                        dec_seq):
    rows = (GROUP // 2) * dec_seq
    key_pad = jnp.zeros((WINDOW - SAMPLE_KEY_PAD, KV_WIDTH), F32)
    bias = bias_ref[...]
    slab = lax.broadcasted_iota(jnp.int32, (rows, 1), 0) // dec_seq
    sinks = []
    for kh in range(N_KV_HEADS):
        per_parity = []
        for parity in range(2):
            sink = jnp.zeros((rows, 1), F32)
            for s in range(GROUP // 2):
                sink = jnp.where(slab == s, sink_ref[kh * GROUP + 2 * s + parity] * LOG2E, sink)
            per_parity.append([sink])
        sinks.append(per_parity)
    for b in range(q_ref.shape[0]):
        kc, vc, kn, vn = kc_ref[b], vc_ref[b], kn_ref[b], vn_ref[b]
        wk_ref[b, 0:WINDOW - dec_seq, :] = kc[dec_seq:, :]
        wk_ref[b, WINDOW - dec_seq:WINDOW, :] = kn[0:dec_seq, :]
        wv_ref[b, 0:WINDOW - dec_seq, :] = vc[dec_seq:, :]
        wv_ref[b, WINDOW - dec_seq:WINDOW, :] = vn[0:dec_seq, :]
        k2 = jnp.concatenate([kc, kn, key_pad], axis=0)
        v2 = jnp.concatenate([vc, vn, key_pad], axis=0)
        for kh in range(N_KV_HEADS):
            keys, vals = _kv_tiles(k2, v2, kh)
            q_rows = q_ref[b, kh * rows:(kh + 1) * rows, :]
            (out,) = _kv_attention([q_rows], sinks[kh], keys, vals, bias)
            o_ref[b, kh * rows:(kh + 1) * rows, :] = out.astype(o_ref.dtype)


def _attn_sample(sinks, q, k_new, v_new, k_cache, v_cache, *, dec_seq):
    nb = q.shape[0]
    bt = SAMPLE_BATCH_TILE
    blk = lambda rows: pl.BlockSpec((bt, rows, LANES), lambda i: (i, 0, 0))
    q_rows = N_KV_HEADS * (GROUP // 2) * dec_seq
    bias = jnp.asarray(_band_bias(np.arange((GROUP // 2) * dec_seq) % dec_seq, False))
    return pl.pallas_call(
        functools.partial(_attn_sample_kernel, dec_seq=dec_seq),
        grid=(nb // bt,),
        in_specs=[pl.BlockSpec(memory_space=pltpu.SMEM), _resident(bias.shape), blk(q_rows),
                  blk(SAMPLE_KEY_PAD), blk(SAMPLE_KEY_PAD), blk(WINDOW), blk(WINDOW)],
        out_specs=[blk(q_rows), blk(WINDOW), blk(WINDOW)],
        out_shape=[jax.ShapeDtypeStruct((nb, q_rows, LANES), BF16),
                   jax.ShapeDtypeStruct((nb, WINDOW, KV_WIDTH), F32),
                   jax.ShapeDtypeStruct((nb, WINDOW, KV_WIDTH), F32)],
        compiler_params=pltpu.CompilerParams(
            dimension_semantics=("arbitrary",), vmem_limit_bytes=VMEM_LIMIT_BYTES),
        name="attn_sample",
    )(sinks, bias, q, k_new, v_new, k_cache, v_cache)


def _mix_ffn_kernel(sink_ref, bias_ref, h_ref, a_ref, q_ref, kp_ref, k_ref, vp_ref, v_ref, bs_ref,
                    wg_ref, wpa_ref, wpb_ref, wo_ref, g2_ref, b2_ref, wup_ref, wdn_ref, g3_ref, b3_ref,
                    ys_ref, yp_ref, b_buf, *, tiles_per_seq):
    step = pl.program_id(0)
    first_of_sequence = (step - 1) % tiles_per_seq == 0
    h = h_ref[...]
    hb = h.astype(BF16)
    a = a_ref[...]
    gated_a, gates_b = [], []
    for blk in range(h_ref.shape[0] // WINDOW):
        if blk == 0:
            bias = bias_ref[jnp.where(first_of_sequence, 1, 0)]
            k_prev, v_prev = kp_ref[...], vp_ref[...]
        else:
            bias = bias_ref[0]
            k_prev, v_prev = k_ref[(blk - 1) * WINDOW:blk * WINDOW, :], v_ref[(blk - 1) * WINDOW:blk * WINDOW, :]
        _attend_block(blk, sink_ref, bias, q_ref, k_prev, v_prev, k_ref, v_ref, b_buf)
        cols = slice(blk * PROJ_GROUP, (blk + 1) * PROJ_GROUP)
        gate_b_cols = slice(D_MODEL + cols.start, D_MODEL + cols.stop)
        gate_a = jax.nn.sigmoid(jnp.dot(hb, wg_ref[:, cols], preferred_element_type=F32))
        gated_a.append(gate_a * jnp.dot(a, wpa_ref[:, cols], preferred_element_type=F32))
        gates_b.append(jax.nn.sigmoid(jnp.dot(hb, wg_ref[:, gate_b_cols], preferred_element_type=F32)))
    b = jnp.where(step == 0, bs_ref[...], b_buf[...])
    merged = []
    for blk, (part, gate_b) in enumerate(zip(gated_a, gates_b)):
        cols = slice(blk * PROJ_GROUP, (blk + 1) * PROJ_GROUP)
        part = part + gate_b * jnp.dot(b, wpb_ref[:, cols], preferred_element_type=F32)
        merged.append(part.astype(BF16))
    y = jnp.dot(jnp.concatenate(merged, axis=1), wo_ref[...], preferred_element_type=F32)
    h2 = _layernorm(DN_ALPHA * h + y, g2_ref[...], b2_ref[...])
    _store_rows((ys_ref, yp_ref), _ffn_ln_rows(h2, wup_ref, wdn_ref, g3_ref, b3_ref))


def _mix_ffn(sinks, h, a_out, q, k, v, b_sample, w_gate, w_pa, w_pb, w_o, g2, b2, w_up, w_down, g3, b3, *, tm,
             tiles_per_seq):
    t = h.shape[0]
    assert tm // WINDOW == D_MODEL // PROJ_GROUP
    bias = jnp.asarray(np.stack([_band_bias(np.arange(WINDOW), False), _band_bias(np.arange(WINDOW), True)]))
    prev = pl.BlockSpec((WINDOW, KV_WIDTH), lambda i: (jnp.maximum(i * (tm // WINDOW) - 1, 0), 0))
    return pl.pallas_call(
        functools.partial(_mix_ffn_kernel, tiles_per_seq=tiles_per_seq),
        grid=(t // tm,),
        in_specs=[pl.BlockSpec(memory_space=pltpu.SMEM), _resident(bias.shape),
                  _row_spec(tm, D_MODEL), _row_spec(tm, A_WIDTH), _row_spec(tm, Q_WIDTH),
                  prev, _row_spec(tm, KV_WIDTH), prev, _row_spec(tm, KV_WIDTH),
                  pl.BlockSpec((tm, Q_WIDTH), lambda i: (0, 0), pipeline_mode=pl.Buffered(1)),
                  _resident(w_gate.shape), _resident(w_pa.shape), _resident(w_pb.shape), _resident(w_o.shape),
                  _resident(g2.shape), _resident(b2.shape), _resident(w_up.shape), _resident(w_down.shape),
                  _resident(g3.shape), _resident(b3.shape)],
        out_specs=[*_stream_specs(tm, D_MODEL)],
        out_shape=[jax.ShapeDtypeStruct((tm, D_MODEL), F32), jax.ShapeDtypeStruct((t - tm, D_MODEL), F32)],
        scratch_shapes=[pltpu.VMEM((tm, Q_WIDTH), BF16)],
        compiler_params=pltpu.CompilerParams(
            dimension_semantics=("arbitrary",), vmem_limit_bytes=VMEM_LIMIT_BYTES),
        name="mix_ffn",
    )(sinks, bias, h, a_out, q, k, k, v, v, b_sample, w_gate, w_pa, w_pb, w_o, g2, b2, w_up, w_down, g3, b3)


def _rope_tables(positions):
    half = HEAD_DIM // 2
    inv = ROPE_THETA ** (-np.arange(half, dtype=np.float64) / half)
    ang = np.asarray(positions, np.float64)[:, None] * inv[None, :]
    cos = np.tile(np.cos(ang), (1, 2 * LANES // HEAD_DIM))
    sin = np.tile(np.concatenate([-np.sin(ang), np.sin(ang)], axis=1), (1, LANES // HEAD_DIM))
    q_scale = LOG2E * HEAD_DIM ** -0.5
    return jnp.asarray(np.concatenate([cos, sin, cos * q_scale, sin * q_scale], axis=1), F32)


def kernel(x_prompt, x_sample, cache_win_k, cache_win_v, ffn1_up, ffn1_down, ln1_g, ln1_b, w_in, a_ln_g, a_ln_b,
           a_ws, a_bs, attn_sinks, w_pa, w_pb, w_o, ln2_g, ln2_b, ffn2_up, ffn2_down, ln3_g, ln3_b):
    batch, seq, _ = x_prompt.shape
    dec_batch, dec_seq, _ = x_sample.shape
    buf = cache_win_k.shape[2]
    assert DEPTH == 1 and buf == WINDOW and seq % ROW_TILE == 0 and ROW_TILE % CHUNK == 0
    assert CHUNK % dec_seq == 0 and (dec_batch * dec_seq) % CHUNK == 0 and dec_batch % SAMPLE_BATCH_TILE == 0
    assert dec_seq & (dec_seq - 1) == 0
    tp, ts = batch * seq, dec_batch * dec_seq
    assert ts == ROW_TILE
    assert seq % CHUNK == 0 and CHUNK == WINDOW

    up1, dn1 = ffn1_up[0].astype(BF16), ffn1_down[0].astype(BF16)
    sinks = attn_sinks[0]
    mix = a_ws[0]
    mix_bias = jnp.repeat(a_bs[0].T, A_GROUP_DIM, axis=1)
    rope = _rope_tables(np.concatenate([PAST_LEN + np.arange(ts) % dec_seq, np.arange(seq)]))

    xs, xp = x_sample.reshape(ts, D_MODEL), x_prompt.reshape(tp, D_MODEL)
    whole = lambda w: (w, 0, w.shape[1])
    cache_t = lambda c: c[0].transpose(0, 2, 3, 1).reshape(dec_batch, KV_WIDTH, buf)
    (h,), (up2, dn2, w_mix, w_gate, wpa, wpb, wo, k_cache, v_cache) = _ffn_ln(
        (xs, xp), up1, dn1, ln1_g, ln1_b, tm=ROW_TILE, out_rows=ts + tp,
        cast=(whole(ffn2_up[0]), whole(ffn2_down[0]), (w_in[0], 0, MIX_WIDTH), (w_in[0], MIX_WIDTH, IN_WIDTH),
              whole(w_pa[0]), whole(w_pb[0]), whole(w_o[0])),
        transpose=(cache_t(cache_win_k), cache_t(cache_win_v)))
    a_out, q, k, v, vn_first, vn_tail, k_tail, v_tail = _in_proj(
        h, w_mix, a_ln_g, a_ln_b, mix, mix_bias, rope, tm=ROW_TILE, table_blocks=seq // ROW_TILE,
        sample_tokens=dec_seq)

    slabs = Q_WIDTH // LANES
    qs_t = q[:ts].reshape(dec_batch, dec_seq, slabs, LANES).transpose(0, 2, 1, 3).reshape(
        dec_batch, slabs * dec_seq, LANES)
    pad = ((0, 0), (0, SAMPLE_KEY_PAD - dec_seq), (0, 0))
    ks_pad = jnp.pad(k[:ts].reshape(dec_batch, dec_seq, KV_WIDTH), pad)
    vs_pad = jnp.pad(v[:ts].reshape(dec_batch, dec_seq, KV_WIDTH), pad)
    bs_t, win_k_s, win_v_s = _attn_sample(
        sinks, qs_t, ks_pad, vs_pad, k_cache, v_cache, dec_seq=dec_seq)
    b_sample = bs_t.reshape(dec_batch, slabs, dec_seq, LANES).transpose(0, 2, 1, 3).reshape(ts, Q_WIDTH)

    ys, yp = _mix_ffn(sinks, h, a_out, q, k, v, b_sample, w_gate, wpa, wpb, wo, ln2_g, ln2_b, up2, dn2, ln3_g, ln3_b,
                      tm=ROW_TILE, tiles_per_seq=seq // ROW_TILE)

    kv_shape = (DEPTH, -1, buf, N_KV_HEADS, HEAD_DIM)
    return (yp.reshape(batch, seq, D_MODEL), ys.reshape(dec_batch, dec_seq, D_MODEL),
            k_tail.reshape(kv_shape), v_tail.reshape(kv_shape),
            win_k_s.reshape(kv_shape), win_v_s.reshape(kv_shape),
            vn_tail[None], vn_first[0].reshape(DEPTH, dec_batch, dec_seq, A_WIDTH))
```

```python
import functools

import numpy as np
import jax
import jax.numpy as jnp
from jax import lax
from jax.experimental import pallas as pl
from jax.experimental.pallas import tpu as pltpu

F32 = jnp.float32
BF16 = jnp.bfloat16

D_MODEL = 1024
DEPTH = 1
PAST_LEN = 16384
N_HEADS = 16
N_KV_HEADS = 2
HEAD_DIM = 64
GROUP = N_HEADS // N_KV_HEADS
WINDOW = 128
ROPE_THETA = 10000.0
CHUNK = 128
A_GROUPS = 4
A_GROUP_DIM = 128
A_WIDTH = A_GROUPS * A_GROUP_DIM
D_FF = 2816
LN_EPS = 1e-5
NEG_INF = -1e30
DN_ALPHA = (2.0 * DEPTH) ** 0.25
LOG2E = 1.4426950408889634
Q_WIDTH = N_HEADS * HEAD_DIM
KV_WIDTH = N_KV_HEADS * HEAD_DIM
MIX_WIDTH = 2 * A_WIDTH + Q_WIDTH + 2 * KV_WIDTH
IN_WIDTH = MIX_WIDTH + 2 * D_MODEL

LANES = 128
BF16_TILE_ROWS = 16
VMEM_BYTES = 64 * 1024 * 1024
VMEM_LIMIT_BYTES = VMEM_BYTES - 4 * 1024 * 1024

ROW_TILE = 512
FF_CHUNK = 256
PROJ_GROUP = 256
SAMPLE_KEY_PAD = 16
SAMPLE_BATCH_TILE = 16


def _resident(shape):
    nd = len(shape)
    return pl.BlockSpec(shape, lambda *_: (0,) * nd, pipeline_mode=pl.Buffered(1))


def _row_spec(tm, width):
    return pl.BlockSpec((tm, width), lambda i: (i, 0))


def _stream_specs(tm, width):
    return (pl.BlockSpec((tm, width), lambda i: (0, 0), pipeline_mode=pl.Buffered(1)),
            pl.BlockSpec((tm, width), lambda i: (jnp.maximum(i - 1, 0), 0)))


def _load_rows(refs):
    if len(refs) == 1:
        return refs[0][...]
    return jnp.where(pl.program_id(0) == 0, refs[0][...], refs[1][...])


def _store_rows(refs, value):
    if len(refs) == 1:
        refs[0][...] = value.astype(refs[0].dtype)
        return
    sample_ref, prompt_ref = refs
    prompt_ref[...] = value.astype(prompt_ref.dtype)

    @pl.when(pl.program_id(0) == 0)
    def _():
        sample_ref[...] = prompt_ref[...]


def _layernorm(y, g, b):
    mu = jnp.mean(y, axis=-1, keepdims=True)
    d = y - mu
    var = jnp.mean(d * d, axis=-1, keepdims=True)
    return d * lax.rsqrt(var + LN_EPS) * g + b


def _ffn_ln_rows(x, wup_ref, wdn_ref, g_ref, b_ref):
    xb = x.astype(BF16)
    y = None
    for c in range(D_FF // FF_CHUNK):
        lo = c * FF_CHUNK
        gate = jnp.dot(xb, wup_ref[:, lo:lo + FF_CHUNK].astype(BF16), preferred_element_type=F32)
        up = jnp.dot(xb, wup_ref[:, D_FF + lo:D_FF + lo + FF_CHUNK].astype(BF16), preferred_element_type=F32)
        act = (gate * jax.nn.sigmoid(gate) * up).astype(BF16)
        part = jnp.dot(act, wdn_ref[lo:lo + FF_CHUNK, :].astype(BF16), preferred_element_type=F32)
        y = part if y is None else y + part
    return _layernorm(DN_ALPHA * x + 0.5 * y, g_ref[...], b_ref[...])


def _ffn_ln_kernel(*refs, n_in, n_out, cast_cols, n_transpose):
    x_refs, (wup_ref, wdn_ref, g_ref, b_ref), rest = refs[:n_in], refs[n_in:n_in + 4], refs[n_in + 4:]
    n_side = len(cast_cols) + n_transpose
    side_in, o_refs, side_out = rest[:n_side], rest[n_side:n_side + n_out], rest[n_side + n_out:]
    x = _load_rows(x_refs).astype(F32)
    _store_rows(o_refs, _ffn_ln_rows(x, wup_ref, wdn_ref, g_ref, b_ref))
    for src, dst, (lo, hi) in zip(side_in, side_out, cast_cols):
        dst[...] = src[:, lo:hi].astype(dst.dtype)
    for src, dst in zip(side_in[len(cast_cols):], side_out[len(cast_cols):]):
        for j in range(src.shape[0]):
            dst[j] = src[j].T


def _cast_chunk_rows(rows, steps):
    return next(c for c in range(BF16_TILE_ROWS, rows + 1, BF16_TILE_ROWS)
                if rows % c == 0 and rows // c <= steps)


def _ffn_ln(xs, w_up, w_down, g, b, *, tm, out_rows, cast=(), transpose=()):
    xs = xs if isinstance(xs, tuple) else (xs,)
    outs = out_rows if isinstance(out_rows, tuple) else (out_rows,)
    steps = sum(outs) // tm
    pick = lambda n: _stream_specs(tm, D_MODEL) if n == 2 else (_row_spec(tm, D_MODEL),)
    side_in_specs, side_out_specs, side_shapes = [], [], []
    for w, lo, hi in cast:
        chunk = _cast_chunk_rows(w.shape[0], steps)
        rows = lambda i, last=w.shape[0] // chunk - 1: (jnp.minimum(i, last), 0)
        side_in_specs.append(pl.BlockSpec((chunk, w.shape[1]), rows))
        side_out_specs.append(pl.BlockSpec((chunk, hi - lo), rows))
        side_shapes.append(jax.ShapeDtypeStruct((w.shape[0], hi - lo), BF16))
    for t in transpose:
        batches, r, c = t.shape
        per_step = next(n for n in range(1, batches + 1) if batches % n == 0 and batches // n <= steps)
        index = lambda i, last=batches // per_step - 1: (jnp.minimum(i, last), 0, 0)
        side_in_specs.append(pl.BlockSpec((per_step, r, c), index))
        side_out_specs.append(pl.BlockSpec((per_step, c, r), index))
        side_shapes.append(jax.ShapeDtypeStruct((batches, c, r), t.dtype))
    out = pl.pallas_call(
        functools.partial(_ffn_ln_kernel, n_in=len(xs), n_out=len(outs),
                          cast_cols=tuple((lo, hi) for _, lo, hi in cast), n_transpose=len(transpose)),
        grid=(steps,),
        in_specs=[*pick(len(xs)), _resident(w_up.shape), _resident(w_down.shape),
                  _resident(g.shape), _resident(b.shape)] + side_in_specs,
        out_specs=[*pick(len(outs))] + side_out_specs,
        out_shape=[jax.ShapeDtypeStruct((r, D_MODEL), F32) for r in outs] + side_shapes,
        compiler_params=pltpu.CompilerParams(
            dimension_semantics=("arbitrary",), vmem_limit_bytes=VMEM_LIMIT_BYTES),
        name="ffn_ln",
    )(*xs, w_up, w_down, g, b, *[w for w, _, _ in cast], *transpose)
    return tuple(out[:len(outs)]), tuple(out[len(outs):])


def _rope(x, cos, sin_signed, first_half):
    partner = jnp.where(first_half, pltpu.roll(x, LANES - HEAD_DIM // 2, axis=1),
                        pltpu.roll(x, HEAD_DIM // 2, axis=1))
    return x * cos + partner * sin_signed


def _spatial_mix(mix_ref, bias_ref, chunk_tokens):
    r = lax.broadcasted_iota(jnp.int32, (CHUNK, CHUNK), 0)
    c = lax.broadcasted_iota(jnp.int32, (CHUNK, CHUNK), 1)
    if chunk_tokens == CHUNK:
        return tuple(jnp.where(c <= r, mix_ref[g], 0.0).astype(BF16) for g in range(A_GROUPS)), bias_ref[...]
    token_mask = chunk_tokens - 1
    pick = jnp.where(((r & token_mask) == c) & (c < chunk_tokens), 1.0, 0.0).astype(BF16)
    keep = (c <= r) & ((c | token_mask) == (r | token_mask))
    mixes = []
    for g in range(A_GROUPS):
        rows_of_token = jnp.dot(pick, mix_ref[g].astype(BF16), preferred_element_type=F32).astype(BF16)
        tiled = lax.dot_general(rows_of_token, pick, (((1,), (1,)), ((), ())), preferred_element_type=F32)
        mixes.append(jnp.where(keep, tiled, 0.0).astype(BF16))
    token = lax.broadcasted_iota(jnp.int32, (CHUNK, A_WIDTH), 0) & token_mask
    bias = jnp.zeros((CHUNK, A_WIDTH), F32)
    for t in range(chunk_tokens):
        bias = jnp.where(token == t, bias_ref[t:t + 1, :], bias)
    return tuple(mixes), bias


def _in_proj_kernel(h_ref, w_ref, lng_ref, lnb_ref, mix_ref, bias_ref, rope_ref,
                    a_ref, q_ref, k_ref, v_ref, vn_first_ref, vn_tail_ref, k_tail_ref, v_tail_ref, *, sample_tokens):
    mixes, bias = lax.cond(pl.program_id(0) == 0,
                           lambda: _spatial_mix(mix_ref, bias_ref, sample_tokens),
                           lambda: _spatial_mix(mix_ref, bias_ref, CHUNK))
    hb = h_ref[...].astype(BF16)
    cos_k, sin_k, cos_q, sin_q = [rope_ref[:, i * LANES:(i + 1) * LANES] for i in range(4)]
    lane = lax.broadcasted_iota(jnp.int32, cos_k.shape, 1)
    first_half = (lane & (HEAD_DIM - 1)) < (HEAD_DIM // 2)

    def project(lo):
        return jnp.dot(hb, w_ref[:, lo:lo + PROJ_GROUP], preferred_element_type=F32)

    def rotary_q(i):
        x = project(2 * A_WIDTH + i * PROJ_GROUP)
        for j in range(PROJ_GROUP // LANES):
            cols = slice(i * PROJ_GROUP + j * LANES, i * PROJ_GROUP + (j + 1) * LANES)
            q_ref[:, cols] = _rope(x[:, j * LANES:(j + 1) * LANES], cos_q, sin_q, first_half).astype(q_ref.dtype)

    gu, gv = [], []
    for i in range(A_WIDTH // PROJ_GROUP):
        gu.append(jax.nn.gelu(project(i * PROJ_GROUP)))
        rotary_q(2 * i)
        gv.append(jax.nn.gelu(project(A_WIDTH + i * PROJ_GROUP)))
        rotary_q(2 * i + 1)
    kv = project(2 * A_WIDTH + Q_WIDTH)
    tail = slice(h_ref.shape[0] - WINDOW, h_ref.shape[0])
    k_ref[...] = _rope(kv[:, :KV_WIDTH], cos_k, sin_k, first_half)
    v_ref[...] = kv[:, KV_WIDTH:]
    k_tail_ref[0] = k_ref[tail, :]
    v_tail_ref[0] = v_ref[tail, :]

    gu = jnp.concatenate(gu, axis=1)
    vn = _layernorm(jnp.concatenate(gv, axis=1), lng_ref[...], lnb_ref[...])
    vn_tail_ref[0] = vn[tail, :]
    vn_first_ref[0] = vn
    vnb = vn.astype(BF16)
    for g in range(A_GROUPS):
        cols = slice(g * A_GROUP_DIM, (g + 1) * A_GROUP_DIM)
        for ch in range(h_ref.shape[0] // CHUNK):
            rows = slice(ch * CHUNK, (ch + 1) * CHUNK)
            mixed = jnp.dot(mixes[g], vnb[rows, cols], preferred_element_type=F32) + bias[:, cols]
            a_ref[rows, cols] = (gu[rows, cols] * mixed).astype(a_ref.dtype)


def _in_proj(h, w_in, a_ln_g, a_ln_b, mix, bias_full, rope, *, tm, table_blocks, sample_tokens):
    t = h.shape[0]
    sequences = (t // tm - 1) // table_blocks
    table = pl.BlockSpec((tm, 4 * LANES), lambda i: (jnp.where(i == 0, 0, 1 + (i - 1) % table_blocks), 0))
    tail = lambda width: pl.BlockSpec((1, WINDOW, width), lambda i: (jnp.maximum(i - 1, 0) // table_blocks, 0, 0))
    return pl.pallas_call(
        functools.partial(_in_proj_kernel, sample_tokens=sample_tokens),
        grid=(t // tm,),
        in_specs=[_row_spec(tm, D_MODEL), _resident(w_in.shape), _resident(a_ln_g.shape), _resident(a_ln_b.shape),
                  _resident(mix.shape), _resident(bias_full.shape), table],
        out_specs=[_row_spec(tm, A_WIDTH), _row_spec(tm, Q_WIDTH), _row_spec(tm, KV_WIDTH), _row_spec(tm, KV_WIDTH),
                   pl.BlockSpec((1, tm, A_WIDTH), lambda i: (jnp.minimum(i, 1), 0, 0)),
                   tail(A_WIDTH), tail(KV_WIDTH), tail(KV_WIDTH)],
        out_shape=[jax.ShapeDtypeStruct((t, A_WIDTH), BF16), jax.ShapeDtypeStruct((t, Q_WIDTH), BF16),
                   jax.ShapeDtypeStruct((t, KV_WIDTH), F32), jax.ShapeDtypeStruct((t, KV_WIDTH), F32),
                   jax.ShapeDtypeStruct((2, tm, A_WIDTH), F32), jax.ShapeDtypeStruct((sequences, WINDOW, A_WIDTH), F32),
                   jax.ShapeDtypeStruct((sequences, WINDOW, KV_WIDTH), F32),
                   jax.ShapeDtypeStruct((sequences, WINDOW, KV_WIDTH), F32)],
        compiler_params=pltpu.CompilerParams(
            dimension_semantics=("arbitrary",), vmem_limit_bytes=VMEM_LIMIT_BYTES),
        name="in_proj",
    )(h, w_in, a_ln_g, a_ln_b, mix, bias_full, rope)


def _kv_tiles(k2, v2, kv_head):
    low = lax.broadcasted_iota(jnp.int32, k2.shape, 1) < HEAD_DIM
    k2r = pltpu.roll(k2, HEAD_DIM, axis=1)
    v2r = pltpu.roll(v2, HEAD_DIM, axis=1)
    if kv_head == 0:
        keys, vals = jnp.where(low, k2, k2r), jnp.where(low, v2, v2r)
    else:
        keys, vals = jnp.where(low, k2r, k2), jnp.where(low, v2r, v2)
    vals = jnp.concatenate([vals, jnp.ones_like(vals)], axis=1)
    return keys.astype(BF16), vals.astype(BF16)


def _kv_attention(q_pieces, sink_pieces, keys, vals, bias):
    n, rows = len(q_pieces), q_pieces[0].shape[0]
    q = jnp.concatenate(q_pieces, axis=0) if n > 1 else q_pieces[0]
    q_low = lax.broadcasted_iota(jnp.int32, q.shape, 1) < HEAD_DIM
    zero = jnp.zeros_like(q)
    q_all = jnp.concatenate([jnp.where(q_low, q, zero), jnp.where(q_low, zero, q)], axis=0)
    s_all = lax.dot_general(q_all, keys, (((1,), (1,)), ((), ())), preferred_element_type=F32)
    probs, sink_terms = [], []
    for parity in range(2):
        for piece in range(n):
            lo = (parity * n + piece) * rows
            s = s_all[lo:lo + rows] + bias
            m = jnp.max(s, axis=-1, keepdims=True)
            probs.append(jnp.exp2(s - m).astype(BF16))
            sink_terms.append(jnp.exp2(sink_pieces[parity][piece] - m))
    o_all = jnp.dot(jnp.concatenate(probs, axis=0), vals, preferred_element_type=F32)
    o_low = lax.broadcasted_iota(jnp.int32, (rows, LANES), 1) < HEAD_DIM
    outs = []
    for piece in range(n):
        even = o_all[piece * rows:(piece + 1) * rows]
        odd = o_all[(n + piece) * rows:(n + piece + 1) * rows]
        num = jnp.where(o_low, even[:, :LANES], odd[:, :LANES])
        den = jnp.where(o_low, even[:, LANES:] + sink_terms[piece], odd[:, LANES:] + sink_terms[n + piece])
        outs.append(num / den)
    return outs


def _attend_block(blk, sink_ref, bias, q_ref, k_prev, v_prev, k_ref, v_ref, o_ref):
    slabs = GROUP // 2
    rows = slice(blk * WINDOW, (blk + 1) * WINDOW)
    k2 = jnp.concatenate([k_prev, k_ref[rows, :]], axis=0)
    v2 = jnp.concatenate([v_prev, v_ref[rows, :]], axis=0)
    for kh in range(N_KV_HEADS):
        keys, vals = _kv_tiles(k2, v2, kh)
        cols = [slice((kh * slabs + s) * LANES, (kh * slabs + s + 1) * LANES) for s in range(slabs)]
        sinks = [[sink_ref[kh * GROUP + 2 * s + parity] * LOG2E for s in range(slabs)] for parity in range(2)]
        outs = _kv_attention([q_ref[rows, c] for c in cols], sinks, keys, vals, bias)
        for c, o in zip(cols, outs):
            o_ref[rows, c] = o.astype(o_ref.dtype)


def _band_bias(query_pos, first_block):
    i = np.asarray(query_pos)[:, None]
    j = np.arange(2 * WINDOW)[None, :]
    prev_ok = (j < WINDOW) & (j > i) & (not first_block)
    own_ok = (j >= WINDOW) & (j - WINDOW <= i)
    return np.where(prev_ok | own_ok, 0.0, NEG_INF).astype(np.float32)


def _attn_sample_kernel(sink_ref, bias_ref, q_ref, kn_ref, vn_ref, kc_ref, vc_ref, o_ref, wk_ref, wv_ref, *,
                        dec_seq):
    rows = (GROUP // 2) * dec_seq
    key_pad = jnp.zeros((WINDOW - SAMPLE_KEY_PAD, KV_WIDTH), F32)
    bias = bias_ref[...]
    slab = lax.broadcasted_iota(jnp.int32, (rows, 1), 0) // dec_seq
    sinks = []
    for kh in range(N_KV_HEADS):
        per_parity = []
        for parity in range(2):
            sink = jnp.zeros((rows, 1), F32)
            for s in range(GROUP // 2):
                sink = jnp.where(slab == s, sink_ref[kh * GROUP + 2 * s + parity] * LOG2E, sink)
            per_parity.append([sink])
        sinks.append(per_parity)
    for b in range(q_ref.shape[0]):
        kc, vc, kn, vn = kc_ref[b], vc_ref[b], kn_ref[b], vn_ref[b]
        wk_ref[b, 0:WINDOW - dec_seq, :] = kc[dec_seq:, :]
        wk_ref[b, WINDOW - dec_seq:WINDOW, :] = kn[0:dec_seq, :]
        wv_ref[b, 0:WINDOW - dec_seq, :] = vc[dec_seq:, :]
        wv_ref[b, WINDOW - dec_seq:WINDOW, :] = vn[0:dec_seq, :]
        k2 = jnp.concatenate([kc, kn, key_pad], axis=0)
        v2 = jnp.concatenate([vc, vn, key_pad], axis=0)
        for kh in range(N_KV_HEADS):
            keys, vals = _kv_tiles(k2, v2, kh)
            q_rows = q_ref[b, kh * rows:(kh + 1) * rows, :]
            (out,) = _kv_attention([q_rows], sinks[kh], keys, vals, bias)
            o_ref[b, kh * rows:(kh + 1) * rows, :] = out.astype(o_ref.dtype)


def _attn_sample(sinks, q, k_new, v_new, k_cache, v_cache, *, dec_seq):
    nb = q.shape[0]
    bt = SAMPLE_BATCH_TILE
    blk = lambda rows: pl.BlockSpec((bt, rows, LANES), lambda i: (i, 0, 0))
    q_rows = N_KV_HEADS * (GROUP // 2) * dec_seq
    bias = jnp.asarray(_band_bias(np.arange((GROUP // 2) * dec_seq) % dec_seq, False))
    return pl.pallas_call(
        functools.partial(_attn_sample_kernel, dec_seq=dec_seq),
        grid=(nb // bt,),
        in_specs=[pl.BlockSpec(memory_space=pltpu.SMEM), _resident(bias.shape), blk(q_rows),
                  blk(SAMPLE_KEY_PAD), blk(SAMPLE_KEY_PAD), blk(WINDOW), blk(WINDOW)],
        out_specs=[blk(q_rows), blk(WINDOW), blk(WINDOW)],
        out_shape=[jax.ShapeDtypeStruct((nb, q_rows, LANES), BF16),
                   jax.ShapeDtypeStruct((nb, WINDOW, KV_WIDTH), F32),
                   jax.ShapeDtypeStruct((nb, WINDOW, KV_WIDTH), F32)],
        compiler_params=pltpu.CompilerParams(
            dimension_semantics=("arbitrary",), vmem_limit_bytes=VMEM_LIMIT_BYTES),
        name="attn_sample",
    )(sinks, bias, q, k_new, v_new, k_cache, v_cache)


def _mix_ffn_kernel(sink_ref, bias_ref, h_ref, a_ref, q_ref, kp_ref, k_ref, vp_ref, v_ref, bs_ref,
                    wg_ref, wpa_ref, wpb_ref, wo_ref, g2_ref, b2_ref, wup_ref, wdn_ref, g3_ref, b3_ref,
                    ys_ref, yp_ref, b_buf, *, tiles_per_seq):
    step = pl.program_id(0)
    first_of_sequence = (step - 1) % tiles_per_seq == 0
    h = h_ref[...]
    hb = h.astype(BF16)
    a = a_ref[...]
    gated_a, gates_b = [], []
    for blk in range(h_ref.shape[0] // WINDOW):
        if blk == 0:
            bias = bias_ref[jnp.where(first_of_sequence, 1, 0)]
            k_prev, v_prev = kp_ref[...], vp_ref[...]
        else:
            bias = bias_ref[0]
            k_prev, v_prev = k_ref[(blk - 1) * WINDOW:blk * WINDOW, :], v_ref[(blk - 1) * WINDOW:blk * WINDOW, :]
        _attend_block(blk, sink_ref, bias, q_ref, k_prev, v_prev, k_ref, v_ref, b_buf)
        cols = slice(blk * PROJ_GROUP, (blk + 1) * PROJ_GROUP)
        gate_b_cols = slice(D_MODEL + cols.start, D_MODEL + cols.stop)
        gate_a = jax.nn.sigmoid(jnp.dot(hb, wg_ref[:, cols], preferred_element_type=F32))
        gated_a.append(gate_a * jnp.dot(a, wpa_ref[:, cols], preferred_element_type=F32))
        gates_b.append(jax.nn.sigmoid(jnp.dot(hb, wg_ref[:, gate_b_cols], preferred_element_type=F32)))
    b = jnp.where(step == 0, bs_ref[...], b_buf[...])
    merged = []
    for blk, (part, gate_b) in enumerate(zip(gated_a, gates_b)):
        cols = slice(blk * PROJ_GROUP, (blk + 1) * PROJ_GROUP)
        part = part + gate_b * jnp.dot(b, wpb_ref[:, cols], preferred_element_type=F32)
        merged.append(part.astype(BF16))
    y = jnp.dot(jnp.concatenate(merged, axis=1), wo_ref[...], preferred_element_type=F32)
    h2 = _layernorm(DN_ALPHA * h + y, g2_ref[...], b2_ref[...])
    _store_rows((ys_ref, yp_ref), _ffn_ln_rows(h2, wup_ref, wdn_ref, g3_ref, b3_ref))


def _mix_ffn(sinks, h, a_out, q, k, v, b_sample, w_gate, w_pa, w_pb, w_o, g2, b2, w_up, w_down, g3, b3, *, tm,
             tiles_per_seq):
    t = h.shape[0]
    assert tm // WINDOW == D_MODEL // PROJ_GROUP
    bias = jnp.asarray(np.stack([_band_bias(np.arange(WINDOW), False), _band_bias(np.arange(WINDOW), True)]))
    prev = pl.BlockSpec((WINDOW, KV_WIDTH), lambda i: (jnp.maximum(i * (tm // WINDOW) - 1, 0), 0))
    return pl.pallas_call(
        functools.partial(_mix_ffn_kernel, tiles_per_seq=tiles_per_seq),
        grid=(t // tm,),
        in_specs=[pl.BlockSpec(memory_space=pltpu.SMEM), _resident(bias.shape),
                  _row_spec(tm, D_MODEL), _row_spec(tm, A_WIDTH), _row_spec(tm, Q_WIDTH),
                  prev, _row_spec(tm, KV_WIDTH), prev, _row_spec(tm, KV_WIDTH),
                  pl.BlockSpec((tm, Q_WIDTH), lambda i: (0, 0), pipeline_mode=pl.Buffered(1)),
                  _resident(w_gate.shape), _resident(w_pa.shape), _resident(w_pb.shape), _resident(w_o.shape),
                  _resident(g2.shape), _resident(b2.shape), _resident(w_up.shape), _resident(w_down.shape),
                  _resident(g3.shape), _resident(b3.shape)],
        out_specs=[*_stream_specs(tm, D_MODEL)],
        out_shape=[jax.ShapeDtypeStruct((tm, D_MODEL), F32), jax.ShapeDtypeStruct((t - tm, D_MODEL), F32)],
        scratch_shapes=[pltpu.VMEM((tm, Q_WIDTH), BF16)],
        compiler_params=pltpu.CompilerParams(
            dimension_semantics=("arbitrary",), vmem_limit_bytes=VMEM_LIMIT_BYTES),
        name="mix_ffn",
    )(sinks, bias, h, a_out, q, k, k, v, v, b_sample, w_gate, w_pa, w_pb, w_o, g2, b2, w_up, w_down, g3, b3)


def _rope_tables(positions):
    half = HEAD_DIM // 2
    inv = ROPE_THETA ** (-np.arange(half, dtype=np.float64) / half)
    ang = np.asarray(positions, np.float64)[:, None] * inv[None, :]
    cos = np.tile(np.cos(ang), (1, 2 * LANES // HEAD_DIM))
    sin = np.tile(np.concatenate([-np.sin(ang), np.sin(ang)], axis=1), (1, LANES // HEAD_DIM))
    q_scale = LOG2E * HEAD_DIM ** -0.5
    return jnp.asarray(np.concatenate([cos, sin, cos * q_scale, sin * q_scale], axis=1), F32)


def kernel(x_prompt, x_sample, cache_win_k, cache_win_v, ffn1_up, ffn1_down, ln1_g, ln1_b, w_in, a_ln_g, a_ln_b,
           a_ws, a_bs, attn_sinks, w_pa, w_pb, w_o, ln2_g, ln2_b, ffn2_up, ffn2_down, ln3_g, ln3_b):
    batch, seq, _ = x_prompt.shape
    dec_batch, dec_seq, _ = x_sample.shape
    buf = cache_win_k.shape[2]
    assert DEPTH == 1 and buf == WINDOW and seq % ROW_TILE == 0 and ROW_TILE % CHUNK == 0
    assert CHUNK % dec_seq == 0 and (dec_batch * dec_seq) % CHUNK == 0 and dec_batch % SAMPLE_BATCH_TILE == 0
    assert dec_seq & (dec_seq - 1) == 0
    tp, ts = batch * seq, dec_batch * dec_seq
    assert ts == ROW_TILE
    assert seq % CHUNK == 0 and CHUNK == WINDOW

    up1, dn1 = ffn1_up[0], ffn1_down[0]
    sinks = attn_sinks[0]
    mix = a_ws[0]
    mix_bias = jnp.repeat(a_bs[0].T, A_GROUP_DIM, axis=1)
    rope = _rope_tables(np.concatenate([PAST_LEN + np.arange(ts) % dec_seq, np.arange(seq)]))

    xs, xp = x_sample.reshape(ts, D_MODEL), x_prompt.reshape(tp, D_MODEL)
    whole = lambda w: (w, 0, w.shape[1])
    cache_t = lambda c: c[0].transpose(0, 2, 3, 1).reshape(dec_batch, KV_WIDTH, buf)
    (h,), (up2, dn2, w_mix, w_gate, wpa, wpb, wo, k_cache, v_cache) = _ffn_ln(
        (xs, xp), up1, dn1, ln1_g, ln1_b, tm=ROW_TILE, out_rows=ts + tp,
        cast=(whole(ffn2_up[0]), whole(ffn2_down[0]), (w_in[0], 0, MIX_WIDTH), (w_in[0], MIX_WIDTH, IN_WIDTH),
              whole(w_pa[0]), whole(w_pb[0]), whole(w_o[0])),
        transpose=(cache_t(cache_win_k), cache_t(cache_win_v)))
    a_out, q, k, v, vn_first, vn_tail, k_tail, v_tail = _in_proj(
        h, w_mix, a_ln_g, a_ln_b, mix, mix_bias, rope, tm=ROW_TILE, table_blocks=seq // ROW_TILE,
        sample_tokens=dec_seq)

    slabs = Q_WIDTH // LANES
    qs_t = q[:ts].reshape(dec_batch, dec_seq, slabs, LANES).transpose(0, 2, 1, 3).reshape(
        dec_batch, slabs * dec_seq, LANES)
    pad = ((0, 0), (0, SAMPLE_KEY_PAD - dec_seq), (0, 0))
    ks_pad = jnp.pad(k[:ts].reshape(dec_batch, dec_seq, KV_WIDTH), pad)
    vs_pad = jnp.pad(v[:ts].reshape(dec_batch, dec_seq, KV_WIDTH), pad)
    bs_t, win_k_s, win_v_s = _attn_sample(
        sinks, qs_t, ks_pad, vs_pad, k_cache, v_cache, dec_seq=dec_seq)
    b_sample = bs_t.reshape(dec_batch, slabs, dec_seq, LANES).transpose(0, 2, 1, 3).reshape(ts, Q_WIDTH)

    ys, yp = _mix_ffn(sinks, h, a_out, q, k, v, b_sample, w_gate, wpa, wpb, wo, ln2_g, ln2_b, up2, dn2, ln3_g, ln3_b,
                      tm=ROW_TILE, tiles_per_seq=seq // ROW_TILE)

    kv_shape = (DEPTH, -1, buf, N_KV_HEADS, HEAD_DIM)
    return (yp.reshape(batch, seq, D_MODEL), ys.reshape(dec_batch, dec_seq, D_MODEL),
            k_tail.reshape(kv_shape), v_tail.reshape(kv_shape),
            win_k_s.reshape(kv_shape), win_v_s.reshape(kv_shape),
            vn_tail[None], vn_first[0].reshape(DEPTH, dec_batch, dec_seq, A_WIDTH))
```

```python
import functools

import numpy as np
import jax
import jax.numpy as jnp
from jax import lax
from jax.experimental import pallas as pl
from jax.experimental.pallas import tpu as pltpu

F32 = jnp.float32
BF16 = jnp.bfloat16

D_MODEL = 1024
DEPTH = 1
PAST_LEN = 16384
N_HEADS = 16
N_KV_HEADS = 2
HEAD_DIM = 64
GROUP = N_HEADS // N_KV_HEADS
WINDOW = 128
ROPE_THETA = 10000.0
CHUNK = 128
A_GROUPS = 4
A_GROUP_DIM = 128
A_WIDTH = A_GROUPS * A_GROUP_DIM
D_FF = 2816
LN_EPS = 1e-5
NEG_INF = -1e30
DN_ALPHA = (2.0 * DEPTH) ** 0.25
LOG2E = 1.4426950408889634
Q_WIDTH = N_HEADS * HEAD_DIM
KV_WIDTH = N_KV_HEADS * HEAD_DIM
MIX_WIDTH = 2 * A_WIDTH + Q_WIDTH + 2 * KV_WIDTH
IN_WIDTH = MIX_WIDTH + 2 * D_MODEL

LANES = 128
SUBLANES = 8
BF16_TILE_ROWS = 16
VMEM_BYTES = 64 * 1024 * 1024
VMEM_LIMIT_BYTES = VMEM_BYTES - 4 * 1024 * 1024

ROW_TILE = 512
FF_CHUNK = 256
PROJ_GROUP = 256
SAMPLE_BATCH_TILE = 16


def _resident(shape):
    nd = len(shape)
    return pl.BlockSpec(shape, lambda *_: (0,) * nd, pipeline_mode=pl.Buffered(1))


def _row_spec(tm, width):
    return pl.BlockSpec((tm, width), lambda i: (i, 0))


def _stream_specs(tm, width):
    return (pl.BlockSpec((tm, width), lambda i: (0, 0), pipeline_mode=pl.Buffered(1)),
            pl.BlockSpec((tm, width), lambda i: (jnp.maximum(i - 1, 0), 0)))


def _load_stream(sample_ref, prompt_ref):
    return jnp.where(pl.program_id(0) == 0, sample_ref[...], prompt_ref[...])


def _store_stream(sample_ref, prompt_ref, value):
    prompt_ref[...] = value.astype(prompt_ref.dtype)

    @pl.when(pl.program_id(0) == 0)
    def _():
        sample_ref[...] = prompt_ref[...]


def _layernorm(y, g, b):
    mu = jnp.mean(y, axis=-1, keepdims=True)
    d = y - mu
    var = jnp.mean(d * d, axis=-1, keepdims=True)
    return d * lax.rsqrt(var + LN_EPS) * g + b


def _ffn_ln_rows(x, wup_ref, wdn_ref, g_ref, b_ref):
    xb = x.astype(BF16)
    y = None
    for c in range(D_FF // FF_CHUNK):
        lo = c * FF_CHUNK
        gate = jnp.dot(xb, wup_ref[:, lo:lo + FF_CHUNK].astype(BF16), preferred_element_type=F32)
        up = jnp.dot(xb, wup_ref[:, D_FF + lo:D_FF + lo + FF_CHUNK].astype(BF16), preferred_element_type=F32)
        act = (gate * jax.nn.sigmoid(gate) * up).astype(BF16)
        part = jnp.dot(act, wdn_ref[lo:lo + FF_CHUNK, :].astype(BF16), preferred_element_type=F32)
        y = part if y is None else y + part
    return _layernorm(DN_ALPHA * x + 0.5 * y, g_ref[...], b_ref[...])


def _ffn_ln_kernel(xs_ref, xp_ref, wup_ref, wdn_ref, g_ref, b_ref, *rest, cast_cols, n_transpose):
    n_side = len(cast_cols) + n_transpose
    side_in, o_ref, side_out = rest[:n_side], rest[n_side], rest[n_side + 1:]
    x = _load_stream(xs_ref, xp_ref).astype(F32)
    o_ref[...] = _ffn_ln_rows(x, wup_ref, wdn_ref, g_ref, b_ref)
    for src, dst, (lo, hi) in zip(side_in, side_out, cast_cols):
        dst[...] = src[:, lo:hi].astype(dst.dtype)
    for src, dst in zip(side_in[len(cast_cols):], side_out[len(cast_cols):]):
        for j in range(src.shape[0]):
            dst[j] = src[j].T


def _cast_chunk_rows(rows, steps):
    return next(c for c in range(BF16_TILE_ROWS, rows + 1, BF16_TILE_ROWS)
                if rows % c == 0 and rows // c <= steps)


def _ffn_ln(x_sample, x_prompt, w_up, w_down, g, b, *, tm, cast=(), transpose=()):
    total_rows = x_sample.shape[0] + x_prompt.shape[0]
    steps = total_rows // tm
    side_in_specs, side_out_specs, side_shapes = [], [], []
    for w, lo, hi in cast:
        chunk = _cast_chunk_rows(w.shape[0], steps)
        rows = lambda i, last=w.shape[0] // chunk - 1: (jnp.minimum(i, last), 0)
        side_in_specs.append(pl.BlockSpec((chunk, w.shape[1]), rows))
        side_out_specs.append(pl.BlockSpec((chunk, hi - lo), rows))
        side_shapes.append(jax.ShapeDtypeStruct((w.shape[0], hi - lo), BF16))
    for t in transpose:
        batches, r, c = t.shape
        per_step = next(n for n in range(1, batches + 1) if batches % n == 0 and batches // n <= steps)
        index = lambda i, last=batches // per_step - 1: (jnp.minimum(i, last), 0, 0)
        side_in_specs.append(pl.BlockSpec((per_step, r, c), index))
        side_out_specs.append(pl.BlockSpec((per_step, c, r), index))
        side_shapes.append(jax.ShapeDtypeStruct((batches, c, r), t.dtype))
    out = pl.pallas_call(
        functools.partial(_ffn_ln_kernel, cast_cols=tuple((lo, hi) for _, lo, hi in cast),
                          n_transpose=len(transpose)),
        grid=(steps,),
        in_specs=[*_stream_specs(tm, D_MODEL), _resident(w_up.shape), _resident(w_down.shape),
                  _resident(g.shape), _resident(b.shape)] + side_in_specs,
        out_specs=[_row_spec(tm, D_MODEL)] + side_out_specs,
        out_shape=[jax.ShapeDtypeStruct((total_rows, D_MODEL), F32)] + side_shapes,
        compiler_params=pltpu.CompilerParams(
            dimension_semantics=("arbitrary",), vmem_limit_bytes=VMEM_LIMIT_BYTES),
        name="ffn_ln",
    )(x_sample, x_prompt, w_up, w_down, g, b, *[w for w, _, _ in cast], *transpose)
    return out[0], tuple(out[1:])


def _rope(x, cos, sin_signed, first_half):
    partner = jnp.where(first_half, pltpu.roll(x, LANES - HEAD_DIM // 2, axis=1),
                        pltpu.roll(x, HEAD_DIM // 2, axis=1))
    return x * cos + partner * sin_signed


def _spatial_mix(mix_ref, bias_ref, chunk_tokens):
    r = lax.broadcasted_iota(jnp.int32, (CHUNK, CHUNK), 0)
    c = lax.broadcasted_iota(jnp.int32, (CHUNK, CHUNK), 1)
    if chunk_tokens == CHUNK:
        return tuple(jnp.where(c <= r, mix_ref[g], 0.0).astype(BF16) for g in range(A_GROUPS)), bias_ref[...]
    token_mask = chunk_tokens - 1
    pick = jnp.where(((r & token_mask) == c) & (c < chunk_tokens), 1.0, 0.0).astype(BF16)
    keep = (c <= r) & ((c | token_mask) == (r | token_mask))
    mixes = []
    for g in range(A_GROUPS):
        rows_of_token = jnp.dot(pick, mix_ref[g].astype(BF16), preferred_element_type=F32).astype(BF16)
        tiled = lax.dot_general(rows_of_token, pick, (((1,), (1,)), ((), ())), preferred_element_type=F32)
        mixes.append(jnp.where(keep, tiled, 0.0).astype(BF16))
    token = lax.broadcasted_iota(jnp.int32, (CHUNK, A_WIDTH), 0) & token_mask
    bias = jnp.zeros((CHUNK, A_WIDTH), F32)
    for t in range(chunk_tokens):
        bias = jnp.where(token == t, bias_ref[t:t + 1, :], bias)
    return tuple(mixes), bias


def _in_proj_kernel(h_ref, w_ref, lng_ref, lnb_ref, mix_ref, bias_ref, rope_ref,
                    a_ref, q_ref, k_ref, v_ref, vn_first_ref, vn_tail_ref, k_tail_ref, v_tail_ref, *, sample_tokens):
    mixes, bias = lax.cond(pl.program_id(0) == 0,
                           lambda: _spatial_mix(mix_ref, bias_ref, sample_tokens),
                           lambda: _spatial_mix(mix_ref, bias_ref, CHUNK))
    hb = h_ref[...].astype(BF16)
    cos_k, sin_k, cos_q, sin_q = [rope_ref[:, i * LANES:(i + 1) * LANES] for i in range(4)]
    lane = lax.broadcasted_iota(jnp.int32, cos_k.shape, 1)
    first_half = (lane & (HEAD_DIM - 1)) < (HEAD_DIM // 2)

    def project(lo):
        return jnp.dot(hb, w_ref[:, lo:lo + PROJ_GROUP], preferred_element_type=F32)

    def rotary_q(i):
        x = project(2 * A_WIDTH + i * PROJ_GROUP)
        for j in range(PROJ_GROUP // LANES):
            cols = slice(i * PROJ_GROUP + j * LANES, i * PROJ_GROUP + (j + 1) * LANES)
            q_ref[:, cols] = _rope(x[:, j * LANES:(j + 1) * LANES], cos_q, sin_q, first_half).astype(q_ref.dtype)

    gu, gv = [], []
    for i in range(A_WIDTH // PROJ_GROUP):
        gu.append(jax.nn.gelu(project(i * PROJ_GROUP)))
        rotary_q(2 * i)
        gv.append(jax.nn.gelu(project(A_WIDTH + i * PROJ_GROUP)))
        rotary_q(2 * i + 1)
    kv = project(2 * A_WIDTH + Q_WIDTH)
    tail = slice(h_ref.shape[0] - WINDOW, h_ref.shape[0])
    k_ref[...] = _rope(kv[:, :KV_WIDTH], cos_k, sin_k, first_half)
    v_ref[...] = kv[:, KV_WIDTH:]
    k_tail_ref[0] = k_ref[tail, :]
    v_tail_ref[0] = v_ref[tail, :]

    gu = jnp.concatenate(gu, axis=1)
    vn = _layernorm(jnp.concatenate(gv, axis=1), lng_ref[...], lnb_ref[...])
    vn_tail_ref[0] = vn[tail, :]
    vn_first_ref[0] = vn
    vnb = vn.astype(BF16)
    for g in range(A_GROUPS):
        cols = slice(g * A_GROUP_DIM, (g + 1) * A_GROUP_DIM)
        for ch in range(h_ref.shape[0] // CHUNK):
            rows = slice(ch * CHUNK, (ch + 1) * CHUNK)
            mixed = jnp.dot(mixes[g], vnb[rows, cols], preferred_element_type=F32) + bias[:, cols]
            a_ref[rows, cols] = (gu[rows, cols] * mixed).astype(a_ref.dtype)


def _in_proj(h, w_in, a_ln_g, a_ln_b, mix, bias_full, rope, *, tm, table_blocks, sample_tokens):
    t = h.shape[0]
    sequences = (t // tm - 1) // table_blocks
    table = pl.BlockSpec((tm, 4 * LANES), lambda i: (jnp.where(i == 0, 0, 1 + (i - 1) % table_blocks), 0))
    tail = lambda width: pl.BlockSpec((1, WINDOW, width), lambda i: (jnp.maximum(i - 1, 0) // table_blocks, 0, 0))
    return pl.pallas_call(
        functools.partial(_in_proj_kernel, sample_tokens=sample_tokens),
        grid=(t // tm,),
        in_specs=[_row_spec(tm, D_MODEL), _resident(w_in.shape), _resident(a_ln_g.shape), _resident(a_ln_b.shape),
                  _resident(mix.shape), _resident(bias_full.shape), table],
        out_specs=[_row_spec(tm, A_WIDTH), _row_spec(tm, Q_WIDTH), _row_spec(tm, KV_WIDTH), _row_spec(tm, KV_WIDTH),
                   pl.BlockSpec((1, tm, A_WIDTH), lambda i: (jnp.minimum(i, 1), 0, 0)),
                   tail(A_WIDTH), tail(KV_WIDTH), tail(KV_WIDTH)],
        out_shape=[jax.ShapeDtypeStruct((t, A_WIDTH), BF16), jax.ShapeDtypeStruct((t, Q_WIDTH), BF16),
                   jax.ShapeDtypeStruct((t, KV_WIDTH), F32), jax.ShapeDtypeStruct((t, KV_WIDTH), F32),
                   jax.ShapeDtypeStruct((2, tm, A_WIDTH), F32), jax.ShapeDtypeStruct((sequences, WINDOW, A_WIDTH), F32),
                   jax.ShapeDtypeStruct((sequences, WINDOW, KV_WIDTH), F32),
                   jax.ShapeDtypeStruct((sequences, WINDOW, KV_WIDTH), F32)],
        compiler_params=pltpu.CompilerParams(
            dimension_semantics=("arbitrary",), vmem_limit_bytes=VMEM_LIMIT_BYTES),
        name="in_proj",
    )(h, w_in, a_ln_g, a_ln_b, mix, bias_full, rope)


def _kv_tiles(k2, v2, kv_head):
    low = lax.broadcasted_iota(jnp.int32, k2.shape, 1) < HEAD_DIM
    k2r = pltpu.roll(k2, HEAD_DIM, axis=1)
    v2r = pltpu.roll(v2, HEAD_DIM, axis=1)
    if kv_head == 0:
        keys, vals = jnp.where(low, k2, k2r), jnp.where(low, v2, v2r)
    else:
        keys, vals = jnp.where(low, k2r, k2), jnp.where(low, v2r, v2)
    vals = jnp.concatenate([vals, jnp.ones_like(vals)], axis=1)
    return keys.astype(BF16), vals.astype(BF16)


def _kv_attention(q_pieces, sink_pieces, keys, vals, bias):
    n, rows = len(q_pieces), q_pieces[0].shape[0]
    q = jnp.concatenate(q_pieces, axis=0) if n > 1 else q_pieces[0]
    q_low = lax.broadcasted_iota(jnp.int32, q.shape, 1) < HEAD_DIM
    zero = jnp.zeros_like(q)
    q_all = jnp.concatenate([jnp.where(q_low, q, zero), jnp.where(q_low, zero, q)], axis=0)
    s_all = lax.dot_general(q_all, keys, (((1,), (1,)), ((), ())), preferred_element_type=F32)
    probs, sink_terms = [], []
    for parity in range(2):
        for piece in range(n):
            lo = (parity * n + piece) * rows
            s = s_all[lo:lo + rows] + bias
            m = jnp.max(s, axis=-1, keepdims=True)
            probs.append(jnp.exp2(s - m).astype(BF16))
            sink_terms.append(jnp.exp2(sink_pieces[parity][piece] - m))
    o_all = jnp.dot(jnp.concatenate(probs, axis=0), vals, preferred_element_type=F32)
    o_low = lax.broadcasted_iota(jnp.int32, (rows, LANES), 1) < HEAD_DIM
    outs = []
    for piece in range(n):
        even = o_all[piece * rows:(piece + 1) * rows]
        odd = o_all[(n + piece) * rows:(n + piece + 1) * rows]
        num = jnp.where(o_low, even[:, :LANES], odd[:, :LANES])
        den = jnp.where(o_low, even[:, LANES:] + sink_terms[piece], odd[:, LANES:] + sink_terms[n + piece])
        outs.append(num / den)
    return outs


def _attend_block(blk, sink_ref, bias, q_ref, k_prev, v_prev, k_ref, v_ref, o_ref):
    slabs = GROUP // 2
    rows = slice(blk * WINDOW, (blk + 1) * WINDOW)
    k2 = jnp.concatenate([k_prev, k_ref[rows, :]], axis=0)
    v2 = jnp.concatenate([v_prev, v_ref[rows, :]], axis=0)
    for kh in range(N_KV_HEADS):
        keys, vals = _kv_tiles(k2, v2, kh)
        cols = [slice((kh * slabs + s) * LANES, (kh * slabs + s + 1) * LANES) for s in range(slabs)]
        sinks = [[sink_ref[kh * GROUP + 2 * s + parity] * LOG2E for s in range(slabs)] for parity in range(2)]
        outs = _kv_attention([q_ref[rows, c] for c in cols], sinks, keys, vals, bias)
        for c, o in zip(cols, outs):
            o_ref[rows, c] = o.astype(o_ref.dtype)


def _band_bias(query_pos, first_block):
    i = np.asarray(query_pos)[:, None]
    j = np.arange(2 * WINDOW)[None, :]
    prev_ok = (j < WINDOW) & (j > i) & (not first_block)
    own_ok = (j >= WINDOW) & (j - WINDOW <= i)
    return np.where(prev_ok | own_ok, 0.0, NEG_INF).astype(np.float32)


def _attn_sample_kernel(sink_ref, bias_ref, q_ref, kn_ref, vn_ref, kc_ref, vc_ref, o_ref, wk_ref, wv_ref, *,
                        dec_seq):
    rows = (GROUP // 2) * dec_seq
    key_pad = jnp.zeros((WINDOW - SUBLANES, KV_WIDTH), F32)
    new_row = lax.broadcasted_iota(jnp.int32, (SUBLANES, KV_WIDTH), 0) < dec_seq
    bias = bias_ref[...]
    slab = lax.broadcasted_iota(jnp.int32, (rows, 1), 0) // dec_seq
    sinks = []
    for kh in range(N_KV_HEADS):
        per_parity = []
        for parity in range(2):
            sink = jnp.zeros((rows, 1), F32)
            for s in range(GROUP // 2):
                sink = jnp.where(slab == s, sink_ref[kh * GROUP + 2 * s + parity] * LOG2E, sink)
            per_parity.append([sink])
        sinks.append(per_parity)
    for b in range(q_ref.shape[0]):
        kc, vc = kc_ref[b], vc_ref[b]
        group = slice(b * dec_seq // SUBLANES * SUBLANES, (b * dec_seq // SUBLANES + 1) * SUBLANES)
        shift = -(b * dec_seq) % SUBLANES
        kn, vn = kn_ref[group, :], vn_ref[group, :]
        if shift:
            kn, vn = pltpu.roll(kn, shift, axis=0), pltpu.roll(vn, shift, axis=0)
        kn, vn = jnp.where(new_row, kn, 0.0), jnp.where(new_row, vn, 0.0)
        wk_ref[b, 0:WINDOW - dec_seq, :] = kc[dec_seq:, :]
        wk_ref[b, WINDOW - dec_seq:WINDOW, :] = kn[0:dec_seq, :]
        wv_ref[b, 0:WINDOW - dec_seq, :] = vc[dec_seq:, :]
        wv_ref[b, WINDOW - dec_seq:WINDOW, :] = vn[0:dec_seq, :]
        k2 = jnp.concatenate([kc, kn, key_pad], axis=0)
        v2 = jnp.concatenate([vc, vn, key_pad], axis=0)
        for kh in range(N_KV_HEADS):
            keys, vals = _kv_tiles(k2, v2, kh)
            q_rows = q_ref[b, kh * rows:(kh + 1) * rows, :]
            (out,) = _kv_attention([q_rows], sinks[kh], keys, vals, bias)
            o_ref[b, kh * rows:(kh + 1) * rows, :] = out.astype(o_ref.dtype)


def _attn_sample(sinks, q, k, v, k_cache, v_cache, *, dec_seq):
    nb = q.shape[0]
    bt = SAMPLE_BATCH_TILE
    blk = lambda rows: pl.BlockSpec((bt, rows, LANES), lambda i: (i, 0, 0))
    new_rows = pl.BlockSpec((bt * dec_seq, KV_WIDTH), lambda i: (i, 0))
    q_rows = N_KV_HEADS * (GROUP // 2) * dec_seq
    bias = jnp.asarray(_band_bias(np.arange((GROUP // 2) * dec_seq) % dec_seq, False))
    return pl.pallas_call(
        functools.partial(_attn_sample_kernel, dec_seq=dec_seq),
        grid=(nb // bt,),
        in_specs=[pl.BlockSpec(memory_space=pltpu.SMEM), _resident(bias.shape), blk(q_rows),
                  new_rows, new_rows, blk(WINDOW), blk(WINDOW)],
        out_specs=[blk(q_rows), blk(WINDOW), blk(WINDOW)],
        out_shape=[jax.ShapeDtypeStruct((nb, q_rows, LANES), BF16),
                   jax.ShapeDtypeStruct((nb, WINDOW, KV_WIDTH), F32),
                   jax.ShapeDtypeStruct((nb, WINDOW, KV_WIDTH), F32)],
        compiler_params=pltpu.CompilerParams(
            dimension_semantics=("arbitrary",), vmem_limit_bytes=VMEM_LIMIT_BYTES),
        name="attn_sample",
    )(sinks, bias, q, k, v, k_cache, v_cache)


def _mix_ffn_kernel(sink_ref, bias_ref, h_ref, a_ref, q_ref, kp_ref, k_ref, vp_ref, v_ref, bs_ref,
                    wg_ref, wpa_ref, wpb_ref, wo_ref, g2_ref, b2_ref, wup_ref, wdn_ref, g3_ref, b3_ref,
                    ys_ref, yp_ref, b_buf, *, tiles_per_seq):
    step = pl.program_id(0)
    first_of_sequence = (step - 1) % tiles_per_seq == 0
    h = h_ref[...]
    hb = h.astype(BF16)
    a = a_ref[...]
    gated_a, gates_b = [], []
    for blk in range(h_ref.shape[0] // WINDOW):
        if blk == 0:
            bias = bias_ref[jnp.where(first_of_sequence, 1, 0)]
            k_prev, v_prev = kp_ref[...], vp_ref[...]
        else:
            bias = bias_ref[0]
            k_prev, v_prev = k_ref[(blk - 1) * WINDOW:blk * WINDOW, :], v_ref[(blk - 1) * WINDOW:blk * WINDOW, :]
        _attend_block(blk, sink_ref, bias, q_ref, k_prev, v_prev, k_ref, v_ref, b_buf)
        cols = slice(blk * PROJ_GROUP, (blk + 1) * PROJ_GROUP)
        gate_b_cols = slice(D_MODEL + cols.start, D_MODEL + cols.stop)
        gate_a = jax.nn.sigmoid(jnp.dot(hb, wg_ref[:, cols], preferred_element_type=F32))
        gated_a.append(gate_a * jnp.dot(a, wpa_ref[:, cols], preferred_element_type=F32))
        gates_b.append(jax.nn.sigmoid(jnp.dot(hb, wg_ref[:, gate_b_cols], preferred_element_type=F32)))
    b = jnp.where(step == 0, bs_ref[...], b_buf[...])
    merged = []
    for blk, (part, gate_b) in enumerate(zip(gated_a, gates_b)):
        cols = slice(blk * PROJ_GROUP, (blk + 1) * PROJ_GROUP)
        part = part + gate_b * jnp.dot(b, wpb_ref[:, cols], preferred_element_type=F32)
        merged.append(part.astype(BF16))
    y = jnp.dot(jnp.concatenate(merged, axis=1), wo_ref[...], preferred_element_type=F32)
    h2 = _layernorm(DN_ALPHA * h + y, g2_ref[...], b2_ref[...])
    _store_stream(ys_ref, yp_ref, _ffn_ln_rows(h2, wup_ref, wdn_ref, g3_ref, b3_ref))


def _mix_ffn(sinks, h, a_out, q, k, v, b_sample, w_gate, w_pa, w_pb, w_o, g2, b2, w_up, w_down, g3, b3, *, tm,
             tiles_per_seq):
    t = h.shape[0]
    assert tm // WINDOW == D_MODEL // PROJ_GROUP
    bias = jnp.asarray(np.stack([_band_bias(np.arange(WINDOW), False), _band_bias(np.arange(WINDOW), True)]))
    prev = pl.BlockSpec((WINDOW, KV_WIDTH), lambda i: (jnp.maximum(i * (tm // WINDOW) - 1, 0), 0))
    return pl.pallas_call(
        functools.partial(_mix_ffn_kernel, tiles_per_seq=tiles_per_seq),
        grid=(t // tm,),
        in_specs=[pl.BlockSpec(memory_space=pltpu.SMEM), _resident(bias.shape),
                  _row_spec(tm, D_MODEL), _row_spec(tm, A_WIDTH), _row_spec(tm, Q_WIDTH),
                  prev, _row_spec(tm, KV_WIDTH), prev, _row_spec(tm, KV_WIDTH),
                  pl.BlockSpec((tm, Q_WIDTH), lambda i: (0, 0), pipeline_mode=pl.Buffered(1)),
                  _resident(w_gate.shape), _resident(w_pa.shape), _resident(w_pb.shape), _resident(w_o.shape),
                  _resident(g2.shape), _resident(b2.shape), _resident(w_up.shape), _resident(w_down.shape),
                  _resident(g3.shape), _resident(b3.shape)],
        out_specs=[*_stream_specs(tm, D_MODEL)],
        out_shape=[jax.ShapeDtypeStruct((tm, D_MODEL), F32), jax.ShapeDtypeStruct((t - tm, D_MODEL), F32)],
        scratch_shapes=[pltpu.VMEM((tm, Q_WIDTH), BF16)],
        compiler_params=pltpu.CompilerParams(
            dimension_semantics=("arbitrary",), vmem_limit_bytes=VMEM_LIMIT_BYTES),
        name="mix_ffn",
    )(sinks, bias, h, a_out, q, k, k, v, v, b_sample, w_gate, w_pa, w_pb, w_o, g2, b2, w_up, w_down, g3, b3)


def _rope_tables(positions):
    half = HEAD_DIM // 2
    inv = ROPE_THETA ** (-np.arange(half, dtype=np.float64) / half)
    ang = np.asarray(positions, np.float64)[:, None] * inv[None, :]
    cos = np.tile(np.cos(ang), (1, 2 * LANES // HEAD_DIM))
    sin = np.tile(np.concatenate([-np.sin(ang), np.sin(ang)], axis=1), (1, LANES // HEAD_DIM))
    q_scale = LOG2E * HEAD_DIM ** -0.5
    return jnp.asarray(np.concatenate([cos, sin, cos * q_scale, sin * q_scale], axis=1), F32)


def kernel(x_prompt, x_sample, cache_win_k, cache_win_v, ffn1_up, ffn1_down, ln1_g, ln1_b, w_in, a_ln_g, a_ln_b,
           a_ws, a_bs, attn_sinks, w_pa, w_pb, w_o, ln2_g, ln2_b, ffn2_up, ffn2_down, ln3_g, ln3_b):
    batch, seq, _ = x_prompt.shape
    dec_batch, dec_seq, _ = x_sample.shape
    buf = cache_win_k.shape[2]
    assert DEPTH == 1 and buf == WINDOW and seq % ROW_TILE == 0 and ROW_TILE % CHUNK == 0
    assert CHUNK % dec_seq == 0 and (dec_batch * dec_seq) % CHUNK == 0 and dec_batch % SAMPLE_BATCH_TILE == 0
    assert dec_seq & (dec_seq - 1) == 0 and dec_seq <= SUBLANES
    tp, ts = batch * seq, dec_batch * dec_seq
    assert ts == ROW_TILE
    assert seq % CHUNK == 0 and CHUNK == WINDOW

    sinks = attn_sinks[0]
    mix = a_ws[0]
    mix_bias = jnp.repeat(a_bs[0].T, A_GROUP_DIM, axis=1)
    rope = _rope_tables(np.concatenate([PAST_LEN + np.arange(ts) % dec_seq, np.arange(seq)]))

    xs, xp = x_sample.reshape(ts, D_MODEL), x_prompt.reshape(tp, D_MODEL)
    whole = lambda w: (w, 0, w.shape[1])
    cache_t = lambda c: c[0].transpose(0, 2, 3, 1).reshape(dec_batch, KV_WIDTH, buf)
    h, (up2, dn2, w_mix, w_gate, wpa, wpb, wo, k_cache, v_cache) = _ffn_ln(
        xs, xp, ffn1_up[0], ffn1_down[0], ln1_g, ln1_b, tm=ROW_TILE,
        cast=(whole(ffn2_up[0]), whole(ffn2_down[0]), (w_in[0], 0, MIX_WIDTH), (w_in[0], MIX_WIDTH, IN_WIDTH),
              whole(w_pa[0]), whole(w_pb[0]), whole(w_o[0])),
        transpose=(cache_t(cache_win_k), cache_t(cache_win_v)))
    a_out, q, k, v, vn_first, vn_tail, k_tail, v_tail = _in_proj(
        h, w_mix, a_ln_g, a_ln_b, mix, mix_bias, rope, tm=ROW_TILE, table_blocks=seq // ROW_TILE,
        sample_tokens=dec_seq)

    slabs = Q_WIDTH // LANES
    qs_t = q[:ts].reshape(dec_batch, dec_seq, slabs, LANES).transpose(0, 2, 1, 3).reshape(
        dec_batch, slabs * dec_seq, LANES)
    bs_t, win_k_s, win_v_s = _attn_sample(
        sinks, qs_t, k, v, k_cache, v_cache, dec_seq=dec_seq)
    b_sample = bs_t.reshape(dec_batch, slabs, dec_seq, LANES).transpose(0, 2, 1, 3).reshape(ts, Q_WIDTH)

    ys, yp = _mix_ffn(sinks, h, a_out, q, k, v, b_sample, w_gate, wpa, wpb, wo, ln2_g, ln2_b, up2, dn2, ln3_g, ln3_b,
                      tm=ROW_TILE, tiles_per_seq=seq // ROW_TILE)

    kv_shape = (DEPTH, -1, buf, N_KV_HEADS, HEAD_DIM)
    return (yp.reshape(batch, seq, D_MODEL), ys.reshape(dec_batch, dec_seq, D_MODEL),
            k_tail.reshape(kv_shape), v_tail.reshape(kv_shape),
            win_k_s.reshape(kv_shape), win_v_s.reshape(kv_shape),
            vn_tail[None], vn_first[0].reshape(DEPTH, dec_batch, dec_seq, A_WIDTH))
```

```python
import functools

import numpy as np
import jax
import jax.numpy as jnp
from jax import lax
from jax.experimental import pallas as pl
from jax.experimental.pallas import tpu as pltpu

F32 = jnp.float32
BF16 = jnp.bfloat16

D_MODEL = 1024
DEPTH = 1
PAST_LEN = 16384
N_HEADS = 16
N_KV_HEADS = 2
HEAD_DIM = 64
GROUP = N_HEADS // N_KV_HEADS
WINDOW = 128
ROPE_THETA = 10000.0
CHUNK = 128
A_GROUPS = 4
A_GROUP_DIM = 128
A_WIDTH = A_GROUPS * A_GROUP_DIM
D_FF = 2816
LN_EPS = 1e-5
NEG_INF = -1e30
DN_ALPHA = (2.0 * DEPTH) ** 0.25
LOG2E = 1.4426950408889634
Q_WIDTH = N_HEADS * HEAD_DIM
KV_WIDTH = N_KV_HEADS * HEAD_DIM
MIX_WIDTH = 2 * A_WIDTH + Q_WIDTH + 2 * KV_WIDTH
IN_WIDTH = MIX_WIDTH + 2 * D_MODEL

LANES = 128
SUBLANES = 8
BF16_TILE_ROWS = 16
VMEM_BYTES = 64 * 1024 * 1024
VMEM_LIMIT_BYTES = VMEM_BYTES - 4 * 1024 * 1024

ROW_TILE = 512
FF_CHUNK = 256
PROJ_GROUP = 256
SAMPLE_BATCH_TILE = 16


def _resident(shape):
    nd = len(shape)
    return pl.BlockSpec(shape, lambda *_: (0,) * nd, pipeline_mode=pl.Buffered(1))


def _row_spec(tm, width):
    return pl.BlockSpec((tm, width), lambda i: (i, 0))


def _stream_specs(tm, width):
    return (pl.BlockSpec((tm, width), lambda i: (0, 0), pipeline_mode=pl.Buffered(1)),
            pl.BlockSpec((tm, width), lambda i: (jnp.maximum(i - 1, 0), 0)))


def _load_stream(sample_ref, prompt_ref):
    return jnp.where(pl.program_id(0) == 0, sample_ref[...], prompt_ref[...])


def _store_stream(sample_ref, prompt_ref, value):
    prompt_ref[...] = value.astype(prompt_ref.dtype)

    @pl.when(pl.program_id(0) == 0)
    def _():
        sample_ref[...] = prompt_ref[...]


def _layernorm(y, g, b):
    mu = jnp.mean(y, axis=-1, keepdims=True)
    d = y - mu
    var = jnp.mean(d * d, axis=-1, keepdims=True)
    return d * lax.rsqrt(var + LN_EPS) * g + b


def _ffn_ln_rows(x, wup_ref, wdn_ref, g_ref, b_ref):
    xb = x.astype(BF16)
    y = None
    for c in range(D_FF // FF_CHUNK):
        lo = c * FF_CHUNK
        gate = jnp.dot(xb, wup_ref[:, lo:lo + FF_CHUNK].astype(BF16), preferred_element_type=F32)
        up = jnp.dot(xb, wup_ref[:, D_FF + lo:D_FF + lo + FF_CHUNK].astype(BF16), preferred_element_type=F32)
        act = (gate * jax.nn.sigmoid(gate) * up).astype(BF16)
        part = jnp.dot(act, wdn_ref[lo:lo + FF_CHUNK, :].astype(BF16), preferred_element_type=F32)
        y = part if y is None else y + part
    return _layernorm(DN_ALPHA * x + 0.5 * y, g_ref[...], b_ref[...])


def _ffn_ln_kernel(xs_ref, xp_ref, wup_ref, wdn_ref, g_ref, b_ref, *rest, cast_cols, n_transpose):
    n_side = len(cast_cols) + n_transpose
    side_in, o_ref, side_out = rest[:n_side], rest[n_side], rest[n_side + 1:]
    x = _load_stream(xs_ref, xp_ref).astype(F32)
    o_ref[...] = _ffn_ln_rows(x, wup_ref, wdn_ref, g_ref, b_ref)
    for src, dst, (lo, hi) in zip(side_in, side_out, cast_cols):
        dst[...] = src[:, lo:hi].astype(dst.dtype)
    for src, dst in zip(side_in[len(cast_cols):], side_out[len(cast_cols):]):
        for j in range(src.shape[0]):
            dst[j] = src[j].T


def _cast_chunk_rows(rows, steps):
    return next(c for c in range(BF16_TILE_ROWS, rows + 1, BF16_TILE_ROWS)
                if rows % c == 0 and rows // c <= steps)


def _ffn_ln(x_sample, x_prompt, w_up, w_down, g, b, *, tm, cast=(), transpose=()):
    total_rows = x_sample.shape[0] + x_prompt.shape[0]
    steps = total_rows // tm
    side_in_specs, side_out_specs, side_shapes = [], [], []
    for w, lo, hi in cast:
        chunk = _cast_chunk_rows(w.shape[0], steps)
        rows = lambda i, last=w.shape[0] // chunk - 1: (jnp.minimum(i, last), 0)
        side_in_specs.append(pl.BlockSpec((chunk, w.shape[1]), rows))
        side_out_specs.append(pl.BlockSpec((chunk, hi - lo), rows))
        side_shapes.append(jax.ShapeDtypeStruct((w.shape[0], hi - lo), BF16))
    for t in transpose:
        batches, r, c = t.shape
        per_step = next(n for n in range(1, batches + 1) if batches % n == 0 and batches // n <= steps)
        index = lambda i, last=batches // per_step - 1: (jnp.minimum(i, last), 0, 0)
        side_in_specs.append(pl.BlockSpec((per_step, r, c), index))
        side_out_specs.append(pl.BlockSpec((per_step, c, r), index))
        side_shapes.append(jax.ShapeDtypeStruct((batches, c, r), t.dtype))
    out = pl.pallas_call(
        functools.partial(_ffn_ln_kernel, cast_cols=tuple((lo, hi) for _, lo, hi in cast),
                          n_transpose=len(transpose)),
        grid=(steps,),
        in_specs=[*_stream_specs(tm, D_MODEL), _resident(w_up.shape), _resident(w_down.shape),
                  _resident(g.shape), _resident(b.shape)] + side_in_specs,
        out_specs=[_row_spec(tm, D_MODEL)] + side_out_specs,
        out_shape=[jax.ShapeDtypeStruct((total_rows, D_MODEL), F32)] + side_shapes,
        compiler_params=pltpu.CompilerParams(
            dimension_semantics=("arbitrary",), vmem_limit_bytes=VMEM_LIMIT_BYTES),
        name="ffn_ln",
    )(x_sample, x_prompt, w_up, w_down, g, b, *[w for w, _, _ in cast], *transpose)
    return out[0], tuple(out[1:])


def _rope(x, cos, sin_signed, first_half):
    partner = jnp.where(first_half, pltpu.roll(x, LANES - HEAD_DIM // 2, axis=1),
                        pltpu.roll(x, HEAD_DIM // 2, axis=1))
    return x * cos + partner * sin_signed


def _spatial_mix(mix_ref, bias_ref, chunk_tokens):
    r = lax.broadcasted_iota(jnp.int32, (CHUNK, CHUNK), 0)
    c = lax.broadcasted_iota(jnp.int32, (CHUNK, CHUNK), 1)
    if chunk_tokens == CHUNK:
        return tuple(jnp.where(c <= r, mix_ref[g], 0.0).astype(BF16) for g in range(A_GROUPS)), bias_ref[...]
    token_mask = chunk_tokens - 1
    pick = jnp.where(((r & token_mask) == c) & (c < chunk_tokens), 1.0, 0.0).astype(BF16)
    keep = (c <= r) & ((c | token_mask) == (r | token_mask))
    mixes = []
    for g in range(A_GROUPS):
        rows_of_token = jnp.dot(pick, mix_ref[g].astype(BF16), preferred_element_type=F32).astype(BF16)
        tiled = lax.dot_general(rows_of_token, pick, (((1,), (1,)), ((), ())), preferred_element_type=F32)
        mixes.append(jnp.where(keep, tiled, 0.0).astype(BF16))
    token = lax.broadcasted_iota(jnp.int32, (CHUNK, A_WIDTH), 0) & token_mask
    bias = jnp.zeros((CHUNK, A_WIDTH), F32)
    for t in range(chunk_tokens):
        bias = jnp.where(token == t, bias_ref[t:t + 1, :], bias)
    return tuple(mixes), bias


def _in_proj_kernel(h_ref, w_ref, lng_ref, lnb_ref, mix_ref, bias_ref, rope_ref,
                    a_ref, q_ref, k_ref, v_ref, vn_first_ref, vn_tail_ref, k_tail_ref, v_tail_ref, *, sample_tokens):
    mixes, bias = lax.cond(pl.program_id(0) == 0,
                           lambda: _spatial_mix(mix_ref, bias_ref, sample_tokens),
                           lambda: _spatial_mix(mix_ref, bias_ref, CHUNK))
    hb = h_ref[...].astype(BF16)
    cos_k, sin_k, cos_q, sin_q = [rope_ref[:, i * LANES:(i + 1) * LANES] for i in range(4)]
    lane = lax.broadcasted_iota(jnp.int32, cos_k.shape, 1)
    first_half = (lane & (HEAD_DIM - 1)) < (HEAD_DIM // 2)

    def project(lo):
        return jnp.dot(hb, w_ref[:, lo:lo + PROJ_GROUP], preferred_element_type=F32)

    def rotary_q(i):
        x = project(2 * A_WIDTH + i * PROJ_GROUP)
        for j in range(PROJ_GROUP // LANES):
            cols = slice(i * PROJ_GROUP + j * LANES, i * PROJ_GROUP + (j + 1) * LANES)
            q_ref[:, cols] = _rope(x[:, j * LANES:(j + 1) * LANES], cos_q, sin_q, first_half).astype(q_ref.dtype)

    gu, gv = [], []
    for i in range(A_WIDTH // PROJ_GROUP):
        gu.append(jax.nn.gelu(project(i * PROJ_GROUP)))
        rotary_q(2 * i)
        gv.append(jax.nn.gelu(project(A_WIDTH + i * PROJ_GROUP)))
        rotary_q(2 * i + 1)
    kv = project(2 * A_WIDTH + Q_WIDTH)
    tail = slice(h_ref.shape[0] - WINDOW, h_ref.shape[0])
    k_ref[...] = _rope(kv[:, :KV_WIDTH], cos_k, sin_k, first_half)
    v_ref[...] = kv[:, KV_WIDTH:]
    k_tail_ref[0] = k_ref[tail, :]
    v_tail_ref[0] = v_ref[tail, :]

    gu = jnp.concatenate(gu, axis=1)
    vn = _layernorm(jnp.concatenate(gv, axis=1), lng_ref[...], lnb_ref[...])
    vn_tail_ref[0] = vn[tail, :]
    vn_first_ref[0] = vn
    vnb = vn.astype(BF16)
    for g in range(A_GROUPS):
        cols = slice(g * A_GROUP_DIM, (g + 1) * A_GROUP_DIM)
        for ch in range(h_ref.shape[0] // CHUNK):
            rows = slice(ch * CHUNK, (ch + 1) * CHUNK)
            mixed = jnp.dot(mixes[g], vnb[rows, cols], preferred_element_type=F32) + bias[:, cols]
            a_ref[rows, cols] = (gu[rows, cols] * mixed).astype(a_ref.dtype)


def _in_proj(h, w_in, a_ln_g, a_ln_b, mix, bias_full, rope, *, tm, table_blocks, sample_tokens):
    t = h.shape[0]
    sequences = (t // tm - 1) // table_blocks
    table = pl.BlockSpec((tm, 4 * LANES), lambda i: (jnp.where(i == 0, 0, 1 + (i - 1) % table_blocks), 0))
    tail = lambda width: pl.BlockSpec((1, WINDOW, width), lambda i: (jnp.maximum(i - 1, 0) // table_blocks, 0, 0))
    return pl.pallas_call(
        functools.partial(_in_proj_kernel, sample_tokens=sample_tokens),
        grid=(t // tm,),
        in_specs=[_row_spec(tm, D_MODEL), _resident(w_in.shape), _resident(a_ln_g.shape), _resident(a_ln_b.shape),
                  _resident(mix.shape), _resident(bias_full.shape), table],
        out_specs=[_row_spec(tm, A_WIDTH), _row_spec(tm, Q_WIDTH), _row_spec(tm, KV_WIDTH), _row_spec(tm, KV_WIDTH),
                   pl.BlockSpec((1, tm, A_WIDTH), lambda i: (jnp.minimum(i, 1), 0, 0)),
                   tail(A_WIDTH), tail(KV_WIDTH), tail(KV_WIDTH)],
        out_shape=[jax.ShapeDtypeStruct((t, A_WIDTH), BF16), jax.ShapeDtypeStruct((t, Q_WIDTH), BF16),
                   jax.ShapeDtypeStruct((t, KV_WIDTH), F32), jax.ShapeDtypeStruct((t, KV_WIDTH), F32),
                   jax.ShapeDtypeStruct((2, tm, A_WIDTH), F32), jax.ShapeDtypeStruct((sequences, WINDOW, A_WIDTH), F32),
                   jax.ShapeDtypeStruct((sequences, WINDOW, KV_WIDTH), F32),
                   jax.ShapeDtypeStruct((sequences, WINDOW, KV_WIDTH), F32)],
        compiler_params=pltpu.CompilerParams(
            dimension_semantics=("arbitrary",), vmem_limit_bytes=VMEM_LIMIT_BYTES),
        name="in_proj",
    )(h, w_in, a_ln_g, a_ln_b, mix, bias_full, rope)


def _kv_tiles(k2, v2, kv_head):
    low = lax.broadcasted_iota(jnp.int32, k2.shape, 1) < HEAD_DIM
    k2r = pltpu.roll(k2, HEAD_DIM, axis=1)
    v2r = pltpu.roll(v2, HEAD_DIM, axis=1)
    if kv_head == 0:
        keys, vals = jnp.where(low, k2, k2r), jnp.where(low, v2, v2r)
    else:
        keys, vals = jnp.where(low, k2r, k2), jnp.where(low, v2r, v2)
    vals = jnp.concatenate([vals, jnp.ones_like(vals)], axis=1)
    return keys.astype(BF16), vals.astype(BF16)


def _kv_attention(q_pieces, sink_pieces, keys, vals, bias):
    n, rows = len(q_pieces), q_pieces[0].shape[0]
    q = jnp.concatenate(q_pieces, axis=0) if n > 1 else q_pieces[0]
    q_low = lax.broadcasted_iota(jnp.int32, q.shape, 1) < HEAD_DIM
    zero = jnp.zeros_like(q)
    q_all = jnp.concatenate([jnp.where(q_low, q, zero), jnp.where(q_low, zero, q)], axis=0)
    s_all = lax.dot_general(q_all, keys, (((1,), (1,)), ((), ())), preferred_element_type=F32)
    probs, sink_terms = [], []
    for parity in range(2):
        for piece in range(n):
            lo = (parity * n + piece) * rows
            s = s_all[lo:lo + rows] + bias
            m = jnp.max(s, axis=-1, keepdims=True)
            probs.append(jnp.exp2(s - m).astype(BF16))
            sink_terms.append(jnp.exp2(sink_pieces[parity][piece] - m))
    o_all = jnp.dot(jnp.concatenate(probs, axis=0), vals, preferred_element_type=F32)
    o_low = lax.broadcasted_iota(jnp.int32, (rows, LANES), 1) < HEAD_DIM
    outs = []
    for piece in range(n):
        even = o_all[piece * rows:(piece + 1) * rows]
        odd = o_all[(n + piece) * rows:(n + piece + 1) * rows]
        num = jnp.where(o_low, even[:, :LANES], odd[:, :LANES])
        den = jnp.where(o_low, even[:, LANES:] + sink_terms[piece], odd[:, LANES:] + sink_terms[n + piece])
        outs.append(num / den)
    return outs


def _attend_block(blk, sink_ref, bias, q_ref, k_prev, v_prev, k_ref, v_ref, o_ref):
    slabs = GROUP // 2
    rows = slice(blk * WINDOW, (blk + 1) * WINDOW)
    k2 = jnp.concatenate([k_prev, k_ref[rows, :]], axis=0)
    v2 = jnp.concatenate([v_prev, v_ref[rows, :]], axis=0)
    for kh in range(N_KV_HEADS):
        keys, vals = _kv_tiles(k2, v2, kh)
        cols = [slice((kh * slabs + s) * LANES, (kh * slabs + s + 1) * LANES) for s in range(slabs)]
        sinks = [[sink_ref[kh * GROUP + 2 * s + parity] * LOG2E for s in range(slabs)] for parity in range(2)]
        outs = _kv_attention([q_ref[rows, c] for c in cols], sinks, keys, vals, bias)
        for c, o in zip(cols, outs):
            o_ref[rows, c] = o.astype(o_ref.dtype)


def _band_bias(query_pos, first_block):
    i = np.asarray(query_pos)[:, None]
    j = np.arange(2 * WINDOW)[None, :]
    prev_ok = (j < WINDOW) & (j > i) & (not first_block)
    own_ok = (j >= WINDOW) & (j - WINDOW <= i)
    return np.where(prev_ok | own_ok, 0.0, NEG_INF).astype(np.float32)


def _attn_sample_kernel(sink_ref, bias_ref, q_ref, kn_ref, vn_ref, kc_ref, vc_ref, o_ref, wk_ref, wv_ref, *,
                        dec_seq):
    rows = (GROUP // 2) * dec_seq
    key_pad = jnp.zeros((WINDOW - SUBLANES, KV_WIDTH), F32)
    new_row = lax.broadcasted_iota(jnp.int32, (SUBLANES, KV_WIDTH), 0) < dec_seq
    bias = bias_ref[...]
    slab = lax.broadcasted_iota(jnp.int32, (rows, 1), 0) // dec_seq
    sinks = []
    for kh in range(N_KV_HEADS):
        per_parity = []
        for parity in range(2):
            sink = jnp.zeros((rows, 1), F32)
            for s in range(GROUP // 2):
                sink = jnp.where(slab == s, sink_ref[kh * GROUP + 2 * s + parity] * LOG2E, sink)
            per_parity.append([sink])
        sinks.append(per_parity)
    for b in range(q_ref.shape[0]):
        kc, vc = kc_ref[b], vc_ref[b]
        group = slice(b * dec_seq // SUBLANES * SUBLANES, (b * dec_seq // SUBLANES + 1) * SUBLANES)
        shift = -(b * dec_seq) % SUBLANES
        kn, vn = kn_ref[group, :], vn_ref[group, :]
        if shift:
            kn, vn = pltpu.roll(kn, shift, axis=0), pltpu.roll(vn, shift, axis=0)
        kn, vn = jnp.where(new_row, kn, 0.0), jnp.where(new_row, vn, 0.0)
        wk_ref[b, 0:WINDOW - dec_seq, :] = kc[dec_seq:, :]
        wk_ref[b, WINDOW - dec_seq:WINDOW, :] = kn[0:dec_seq, :]
        wv_ref[b, 0:WINDOW - dec_seq, :] = vc[dec_seq:, :]
        wv_ref[b, WINDOW - dec_seq:WINDOW, :] = vn[0:dec_seq, :]
        k2 = jnp.concatenate([kc, kn, key_pad], axis=0)
        v2 = jnp.concatenate([vc, vn, key_pad], axis=0)
        for kh in range(N_KV_HEADS):
            keys, vals = _kv_tiles(k2, v2, kh)
            q_rows = q_ref[b, kh * rows:(kh + 1) * rows, :]
            (out,) = _kv_attention([q_rows], sinks[kh], keys, vals, bias)
            o_ref[b, kh * rows:(kh + 1) * rows, :] = out.astype(o_ref.dtype)


def _attn_sample(sinks, q, k, v, k_cache, v_cache, *, dec_seq):
    nb = q.shape[0]
    bt = SAMPLE_BATCH_TILE
    blk = lambda rows: pl.BlockSpec((bt, rows, LANES), lambda i: (i, 0, 0))
    new_rows = pl.BlockSpec((bt * dec_seq, KV_WIDTH), lambda i: (i, 0))
    q_rows = N_KV_HEADS * (GROUP // 2) * dec_seq
    bias = jnp.asarray(_band_bias(np.arange((GROUP // 2) * dec_seq) % dec_seq, False))
    return pl.pallas_call(
        functools.partial(_attn_sample_kernel, dec_seq=dec_seq),
        grid=(nb // bt,),
        in_specs=[pl.BlockSpec(memory_space=pltpu.SMEM), _resident(bias.shape), blk(q_rows),
                  new_rows, new_rows, blk(WINDOW), blk(WINDOW)],
        out_specs=[blk(q_rows), blk(WINDOW), blk(WINDOW)],
        out_shape=[jax.ShapeDtypeStruct((nb, q_rows, LANES), BF16),
                   jax.ShapeDtypeStruct((nb, WINDOW, KV_WIDTH), F32),
                   jax.ShapeDtypeStruct((nb, WINDOW, KV_WIDTH), F32)],
        compiler_params=pltpu.CompilerParams(
            dimension_semantics=("arbitrary",), vmem_limit_bytes=VMEM_LIMIT_BYTES),
        name="attn_sample",
    )(sinks, bias, q, k, v, k_cache, v_cache)


def _mix_ffn_kernel(sink_ref, bias_ref, h_ref, a_ref, q_ref, kp_ref, k_ref, vp_ref, v_ref, bs_ref,
                    wg_ref, wpa_ref, wpb_ref, wo_ref, g2_ref, b2_ref, wup_hbm, wdn_hbm, g3_ref, b3_ref,
                    ys_ref, yp_ref, b_buf, wup_ref, wdn_ref, weight_sem, *, tiles_per_seq):
    step = pl.program_id(0)

    def ffn_weight_copies():
        return (pltpu.make_async_copy(wup_hbm, wup_ref, weight_sem.at[0]),
                pltpu.make_async_copy(wdn_hbm, wdn_ref, weight_sem.at[1]))

    @pl.when(step == 0)
    def _():
        for copy in ffn_weight_copies():
            copy.start()

    first_of_sequence = (step - 1) % tiles_per_seq == 0
    h = h_ref[...]
    hb = h.astype(BF16)
    a = a_ref[...]
    gated_a, gates_b = [], []
    for blk in range(h_ref.shape[0] // WINDOW):
        if blk == 0:
            bias = bias_ref[jnp.where(first_of_sequence, 1, 0)]
            k_prev, v_prev = kp_ref[...], vp_ref[...]
        else:
            bias = bias_ref[0]
            k_prev, v_prev = k_ref[(blk - 1) * WINDOW:blk * WINDOW, :], v_ref[(blk - 1) * WINDOW:blk * WINDOW, :]
        _attend_block(blk, sink_ref, bias, q_ref, k_prev, v_prev, k_ref, v_ref, b_buf)
        cols = slice(blk * PROJ_GROUP, (blk + 1) * PROJ_GROUP)
        gate_b_cols = slice(D_MODEL + cols.start, D_MODEL + cols.stop)
        gate_a = jax.nn.sigmoid(jnp.dot(hb, wg_ref[:, cols], preferred_element_type=F32))
        gated_a.append(gate_a * jnp.dot(a, wpa_ref[:, cols], preferred_element_type=F32))
        gates_b.append(jax.nn.sigmoid(jnp.dot(hb, wg_ref[:, gate_b_cols], preferred_element_type=F32)))
    b = jnp.where(step == 0, bs_ref[...], b_buf[...])
    merged = []
    for blk, (part, gate_b) in enumerate(zip(gated_a, gates_b)):
        cols = slice(blk * PROJ_GROUP, (blk + 1) * PROJ_GROUP)
        part = part + gate_b * jnp.dot(b, wpb_ref[:, cols], preferred_element_type=F32)
        merged.append(part.astype(BF16))
    y = jnp.dot(jnp.concatenate(merged, axis=1), wo_ref[...], preferred_element_type=F32)
    h2 = _layernorm(DN_ALPHA * h + y, g2_ref[...], b2_ref[...])

    @pl.when(step == 0)
    def _():
        for copy in ffn_weight_copies():
            copy.wait()

    _store_stream(ys_ref, yp_ref, _ffn_ln_rows(h2, wup_ref, wdn_ref, g3_ref, b3_ref))


def _mix_ffn(sinks, h, a_out, q, k, v, b_sample, w_gate, w_pa, w_pb, w_o, g2, b2, w_up, w_down, g3, b3, *, tm,
             tiles_per_seq):
    t = h.shape[0]
    assert tm // WINDOW == D_MODEL // PROJ_GROUP
    bias = jnp.asarray(np.stack([_band_bias(np.arange(WINDOW), False), _band_bias(np.arange(WINDOW), True)]))
    prev = pl.BlockSpec((WINDOW, KV_WIDTH), lambda i: (jnp.maximum(i * (tm // WINDOW) - 1, 0), 0))
    return pl.pallas_call(
        functools.partial(_mix_ffn_kernel, tiles_per_seq=tiles_per_seq),
        grid=(t // tm,),
        in_specs=[pl.BlockSpec(memory_space=pltpu.SMEM), _resident(bias.shape),
                  _row_spec(tm, D_MODEL), _row_spec(tm, A_WIDTH), _row_spec(tm, Q_WIDTH),
                  prev, _row_spec(tm, KV_WIDTH), prev, _row_spec(tm, KV_WIDTH),
                  pl.BlockSpec((tm, Q_WIDTH), lambda i: (0, 0), pipeline_mode=pl.Buffered(1)),
                  _resident(w_gate.shape), _resident(w_pa.shape), _resident(w_pb.shape), _resident(w_o.shape),
                  _resident(g2.shape), _resident(b2.shape),
                  pl.BlockSpec(memory_space=pl.ANY), pl.BlockSpec(memory_space=pl.ANY),
                  _resident(g3.shape), _resident(b3.shape)],
        out_specs=[*_stream_specs(tm, D_MODEL)],
        out_shape=[jax.ShapeDtypeStruct((tm, D_MODEL), F32), jax.ShapeDtypeStruct((t - tm, D_MODEL), F32)],
        scratch_shapes=[pltpu.VMEM((tm, Q_WIDTH), BF16), pltpu.VMEM(w_up.shape, w_up.dtype),
                        pltpu.VMEM(w_down.shape, w_down.dtype), pltpu.SemaphoreType.DMA((2,))],
        compiler_params=pltpu.CompilerParams(
            dimension_semantics=("arbitrary",), vmem_limit_bytes=VMEM_LIMIT_BYTES),
        name="mix_ffn",
    )(sinks, bias, h, a_out, q, k, k, v, v, b_sample, w_gate, w_pa, w_pb, w_o, g2, b2, w_up, w_down, g3, b3)


def _rope_tables(positions):
    half = HEAD_DIM // 2
    inv = ROPE_THETA ** (-np.arange(half, dtype=np.float64) / half)
    ang = np.asarray(positions, np.float64)[:, None] * inv[None, :]
    cos = np.tile(np.cos(ang), (1, 2 * LANES // HEAD_DIM))
    sin = np.tile(np.concatenate([-np.sin(ang), np.sin(ang)], axis=1), (1, LANES // HEAD_DIM))
    q_scale = LOG2E * HEAD_DIM ** -0.5
    return jnp.asarray(np.concatenate([cos, sin, cos * q_scale, sin * q_scale], axis=1), F32)


def kernel(x_prompt, x_sample, cache_win_k, cache_win_v, ffn1_up, ffn1_down, ln1_g, ln1_b, w_in, a_ln_g, a_ln_b,
           a_ws, a_bs, attn_sinks, w_pa, w_pb, w_o, ln2_g, ln2_b, ffn2_up, ffn2_down, ln3_g, ln3_b):
    batch, seq, _ = x_prompt.shape
    dec_batch, dec_seq, _ = x_sample.shape
    buf = cache_win_k.shape[2]
    assert DEPTH == 1 and buf == WINDOW and seq % ROW_TILE == 0 and ROW_TILE % CHUNK == 0
    assert CHUNK % dec_seq == 0 and (dec_batch * dec_seq) % CHUNK == 0 and dec_batch % SAMPLE_BATCH_TILE == 0
    assert dec_seq & (dec_seq - 1) == 0 and dec_seq <= SUBLANES
    tp, ts = batch * seq, dec_batch * dec_seq
    assert ts == ROW_TILE
    assert seq % CHUNK == 0 and CHUNK == WINDOW

    sinks = attn_sinks[0]
    mix = a_ws[0]
    mix_bias = jnp.repeat(a_bs[0].T, A_GROUP_DIM, axis=1)
    rope = _rope_tables(np.concatenate([PAST_LEN + np.arange(ts) % dec_seq, np.arange(seq)]))

    xs, xp = x_sample.reshape(ts, D_MODEL), x_prompt.reshape(tp, D_MODEL)
    whole = lambda w: (w, 0, w.shape[1])
    cache_t = lambda c: c[0].transpose(0, 2, 3, 1).reshape(dec_batch, KV_WIDTH, buf)
    h, (up2, dn2, w_mix, w_gate, wpa, wpb, wo, k_cache, v_cache) = _ffn_ln(
        xs, xp, ffn1_up[0], ffn1_down[0], ln1_g, ln1_b, tm=ROW_TILE,
        cast=(whole(ffn2_up[0]), whole(ffn2_down[0]), (w_in[0], 0, MIX_WIDTH), (w_in[0], MIX_WIDTH, IN_WIDTH),
              whole(w_pa[0]), whole(w_pb[0]), whole(w_o[0])),
        transpose=(cache_t(cache_win_k), cache_t(cache_win_v)))
    a_out, q, k, v, vn_first, vn_tail, k_tail, v_tail = _in_proj(
        h, w_mix, a_ln_g, a_ln_b, mix, mix_bias, rope, tm=ROW_TILE, table_blocks=seq // ROW_TILE,
        sample_tokens=dec_seq)

    slabs = Q_WIDTH // LANES
    qs_t = q[:ts].reshape(dec_batch, dec_seq, slabs, LANES).transpose(0, 2, 1, 3).reshape(
        dec_batch, slabs * dec_seq, LANES)
    bs_t, win_k_s, win_v_s = _attn_sample(
        sinks, qs_t, k, v, k_cache, v_cache, dec_seq=dec_seq)
    b_sample = bs_t.reshape(dec_batch, slabs, dec_seq, LANES).transpose(0, 2, 1, 3).reshape(ts, Q_WIDTH)

    ys, yp = _mix_ffn(sinks, h, a_out, q, k, v, b_sample, w_gate, wpa, wpb, wo, ln2_g, ln2_b, up2, dn2, ln3_g, ln3_b,
                      tm=ROW_TILE, tiles_per_seq=seq // ROW_TILE)

    kv_shape = (DEPTH, -1, buf, N_KV_HEADS, HEAD_DIM)
    return (yp.reshape(batch, seq, D_MODEL), ys.reshape(dec_batch, dec_seq, D_MODEL),
            k_tail.reshape(kv_shape), v_tail.reshape(kv_shape),
            win_k_s.reshape(kv_shape), win_v_s.reshape(kv_shape),
            vn_tail[None], vn_first[0].reshape(DEPTH, dec_batch, dec_seq, A_WIDTH))
```
